```python
import jax
import jax.numpy as jnp
from jax import lax
import numpy as np

D_MODEL = 1024
BATCH = 8
SEQ = 2048
DEPTH = 2
DEC_BATCH = 32
DEC_SEQ = 1
PAST_LEN = 16384
PAGE_SIZE = 128

N_A_LAYERS = DEPTH // 2
N_B_LAYERS = DEPTH - N_A_LAYERS
N_DENSE = (DEPTH + 1) // 2
N_MOE = DEPTH // 2
D_RNN = ((4 * D_MODEL // 3 + 127) // 128) * 128
N_LRU_BLOCKS = 8
LRU_BLOCK = D_RNN // N_LRU_BLOCKS
CONV_W = 4
LRU_C = 8.0
HEAD_DIM = 128
N_HEADS = D_MODEL // HEAD_DIM
QK_DIM = N_HEADS * HEAD_DIM
ATTN_SCALE = HEAD_DIM ** -0.5
Q_BLOCK = 128
D_FF = 3 * D_MODEL
N_EXPERTS = 8
TOP_K = 2
D_FF_EXPERT = 7 * D_MODEL // 2
P_DIM = 256
EPS = 1e-6

kernel_name = 'hawk_fox_yoco_decoder_step'


def rmsnorm(x, g):
    xf = x.astype(jnp.float32)
    y = xf * lax.rsqrt(jnp.mean(xf * xf, axis=-1, keepdims=True) + EPS)
    return (y * g.astype(jnp.float32)).astype(x.dtype)


def swiglu(x, w_gu, w_down):
    gate, up = jnp.split(x @ w_gu, 2, axis=-1)
    return (jax.nn.silu(gate) * up) @ w_down


def block_diag_linear(x, w, b):
    xb = x.reshape(x.shape[:-1] + (N_LRU_BLOCKS, LRU_BLOCK))
    return jnp.einsum('btnc,nce->btne', xb, w).reshape(x.shape) + b


def recurrent_block(xn, conv_buf, h0, w_in, conv_w, conv_b, w_rg, b_rg, w_ig, b_ig, lru_lambda, w_out):
    t_len = xn.shape[1]
    gate, u = jnp.split(xn @ w_in, 2, axis=-1)
    u_pad = jnp.concatenate([conv_buf.astype(u.dtype), u], axis=1)
    uc = conv_b + u_pad[:, 0:t_len] * conv_w[0]
    for j in range(1, CONV_W):
        uc = uc + u_pad[:, j:j + t_len] * conv_w[j]
    new_buf = u_pad[:, t_len:]
    r = jax.nn.sigmoid(block_diag_linear(uc, w_rg, b_rg).astype(jnp.float32))
    i = jax.nn.sigmoid(block_diag_linear(uc, w_ig, b_ig).astype(jnp.float32))
    log_a = -LRU_C * r * jax.nn.softplus(-lru_lambda.astype(jnp.float32))
    a = jnp.exp(log_a)
    x_in = jnp.sqrt(-jnp.expm1(2.0 * log_a)) * (i * uc.astype(jnp.float32))

    def step(h, inp):
        a_t, x_t = inp
        h = a_t * h + x_t
        return h, h

    h_last, hs = lax.scan(step, h0.astype(jnp.float32),
                          (jnp.swapaxes(a, 0, 1), jnp.swapaxes(x_in, 0, 1)))
    y = jnp.swapaxes(hs, 0, 1).astype(xn.dtype) * jax.nn.gelu(gate)
    return y @ w_out, new_buf, h_last.astype(h0.dtype)


def moe_swiglu(x, w_router, b_router, w_gu_e, w_down_e):
    logits = jnp.einsum('btd,de->bte', x, w_router, preferred_element_type=jnp.float32) + b_router.astype(jnp.float32)
    probs = jax.nn.softmax(logits, axis=-1)
    top_p, top_i = lax.top_k(probs, TOP_K)
    top_p = top_p / jnp.sum(top_p, axis=-1, keepdims=True)
    combine = jnp.einsum('btk,btke->bte', top_p,
                         jax.nn.one_hot(top_i, N_EXPERTS, dtype=jnp.float32)).astype(x.dtype)
    out = combine[..., 0:1] * swiglu(x, w_gu_e[0], w_down_e[0])
    for e in range(1, N_EXPERTS):
        out = out + combine[..., e:e + 1] * swiglu(x, w_gu_e[e], w_down_e[e])
    return out


def add_per_layer_embedding(h, p, g, w_gate, w_proj):
    gate = jax.nn.sigmoid(rmsnorm(h, g) @ w_gate)
    return h + gate * (p @ w_proj)


def shared_kv(h, g_kv, w_kv, b_f):
    bsz, t_len = h.shape[0], h.shape[1]
    kvf = rmsnorm(h, g_kv) @ w_kv
    k = kvf[..., :QK_DIM].reshape(bsz, t_len, N_HEADS, HEAD_DIM)
    v = kvf[..., QK_DIM:2 * QK_DIM].reshape(bsz, t_len, N_HEADS, HEAD_DIM)
    logf = jax.nn.log_sigmoid(kvf[..., 2 * QK_DIM:].astype(jnp.float32) + b_f.astype(jnp.float32))
    return k, v, logf


def fox_attention(q, cq, q_pos, segments):
    bsz, t_q = q.shape[0], q.shape[1]
    n_blk = t_q // Q_BLOCK if t_q % Q_BLOCK == 0 else 1
    q_blk_len = t_q // n_blk
    segs = tuple((k, v, jnp.swapaxes(ck, 1, 2), k_pos) for k, v, ck, k_pos in segments)

    def one_block(args):
        q_b, cq_b, qp_b = args
        cq_t = jnp.swapaxes(cq_b, 1, 2)[..., :, None]
        scores = []
        for k, v, ck_t, k_pos in segs:
            s = jnp.einsum('bqhd,bkhd->bhqk', q_b, k, preferred_element_type=jnp.float32) * ATTN_SCALE
            s = s + (cq_t - ck_t[:, :, None, :])
            scores.append(jnp.where(k_pos[None, :] <= qp_b[:, None], s, -jnp.inf))
        probs = jax.nn.softmax(jnp.concatenate(scores, axis=-1), axis=-1).astype(q.dtype)
        out = None
        start = 0
        for k, v, ck_t, k_pos in segs:
            n = k.shape[1]
            o = jnp.einsum('bhqk,bkhd->bqhd', probs[..., start:start + n], v)
            out = o if out is None else out + o
            start += n
        return out

    qs = jnp.swapaxes(q.reshape(bsz, n_blk, q_blk_len, N_HEADS, HEAD_DIM), 0, 1)
    cqs = jnp.swapaxes(cq.reshape(bsz, n_blk, q_blk_len, N_HEADS), 0, 1)
    qps = q_pos.reshape(n_blk, q_blk_len)
    out = lax.map(one_block, (qs, cqs, qps))
    return jnp.swapaxes(out, 0, 1).reshape(bsz, t_q, QK_DIM)


def run_group(x, p, h0, conv0, past, w):
    bsz, t_len = x.shape[0], x.shape[1]
    h = x
    new_hs, new_bufs = [], []
    kv_ready = False
    for layer in range(DEPTH):
        if layer < N_A_LAYERS:
            a_i = layer
            y, buf, h_last = recurrent_block(
                rmsnorm(h, w['g_mix_a'][a_i]), conv0[a_i], h0[a_i], w['w_in_a'][a_i],
                w['conv_w_a'][a_i], w['conv_b_a'][a_i], w['w_rg_a'][a_i], w['b_rg_a'][a_i],
                w['w_ig_a'][a_i], w['b_ig_a'][a_i], w['lru_lambda_a'][a_i], w['w_out_a'][a_i])
            new_hs.append(h_last)
            new_bufs.append(buf)
            h = h + y
        else:
            b_i = layer - N_A_LAYERS
            if not kv_ready:
                kv_ready = True
                k_new, v_new, logf_new = shared_kv(h, w['g_kv'], w['w_kv'], w['b_f'])
                if past is None:
                    c_all = jnp.cumsum(logf_new, axis=1)
                    q_pos = jnp.arange(t_len)
                    cq = c_all
                    segments = ((k_new, v_new, c_all, q_pos),)
                else:
                    k_past, v_past, logf_past = past
                    t_past = k_past.shape[1]
                    c_all = jnp.cumsum(jnp.concatenate([logf_past.astype(jnp.float32), logf_new], axis=1), axis=1)
                    q_pos = t_past + jnp.arange(t_len)
                    cq = c_all[:, t_past:]
                    segments = ((k_past, v_past, c_all[:, :t_past], jnp.arange(t_past)),
                                (k_new, v_new, cq, q_pos))
            q = (rmsnorm(h, w['g_mix_b'][b_i]) @ w['w_q_b'][b_i]).reshape(bsz, t_len, N_HEADS, HEAD_DIM)
            h = h + fox_attention(q, cq, q_pos, segments) @ w['w_o_b'][b_i]
        hn = rmsnorm(h, w['g_ffn'][layer])
        if layer % 2 == 0:
            d_i = layer // 2
            h = h + swiglu(hn, w['w_gu_dense'][d_i], w['w_down_dense'][d_i])
        else:
            m_i = layer // 2
            h = h + moe_swiglu(hn, w['w_router'][m_i], w['b_router'][m_i], w['w_gu_moe'][m_i], w['w_down_moe'][m_i])
        h = add_per_layer_embedding(h, p[layer], w['g_ple'][layer], w['w_ple_gate'][layer], w['w_ple_proj'][layer])
    y = rmsnorm(h, w['g_final'])
    return y, jnp.stack(new_hs), jnp.stack(new_bufs), k_new, v_new, logf_new


def setup_inputs(seed: int = 0) -> dict:
    key = jax.random.key(seed)
    keys = iter(jax.random.split(key, 48))

    def nrm(shape, scale):
        return jax.random.normal(next(keys), shape, jnp.float32) * scale

    def gain(shape):
        return 1.0 + nrm(shape, 0.02)

    n_pages = PAST_LEN // PAGE_SIZE
    n_used = DEC_BATCH * n_pages
    n_phys = n_used + n_used // 4
    page_table = jax.random.permutation(next(keys), n_phys)[:n_used].reshape(DEC_BATCH, n_pages).astype(jnp.int32)
    u = jax.random.uniform(next(keys), (N_A_LAYERS, D_RNN), jnp.float32, 0.9, 0.999)
    s = u ** (1.0 / LRU_C)
    lru_lambda_a = jnp.log(s) - jnp.log1p(-s)
    b_f = jax.random.uniform(next(keys), (N_HEADS,), jnp.float32, 1.0, 5.0)
    return {
        'x_prompt': nrm((BATCH, SEQ, D_MODEL), 1.0),
        'x_sample': nrm((DEC_BATCH, DEC_SEQ, D_MODEL), 1.0),
        'state_h': nrm((N_A_LAYERS, DEC_BATCH, D_RNN), 0.5),
        'state_conv': nrm((N_A_LAYERS, DEC_BATCH, CONV_W - 1, D_RNN), 1.0),
        'cache_k': nrm((n_phys, PAGE_SIZE, N_HEADS, HEAD_DIM), 1.0),
        'cache_v': nrm((n_phys, PAGE_SIZE, N_HEADS, HEAD_DIM), 1.0),
        'cache_logf': jax.nn.log_sigmoid(3.0 + nrm((n_phys, PAGE_SIZE, N_HEADS), 1.0)),
        'page_table': page_table,
        'p_prompt': nrm((DEPTH, BATCH, SEQ, P_DIM), 1.0),
        'p_sample': nrm((DEPTH, DEC_BATCH, DEC_SEQ, P_DIM), 1.0),
        'g_mix_a': gain((N_A_LAYERS, D_MODEL)),
        'w_in_a': nrm((N_A_LAYERS, D_MODEL, 2 * D_RNN), D_MODEL ** -0.5),
        'conv_w_a': nrm((N_A_LAYERS, CONV_W, D_RNN), CONV_W ** -0.5),
        'conv_b_a': nrm((N_A_LAYERS, D_RNN), 0.01),
        'w_rg_a': nrm((N_A_LAYERS, N_LRU_BLOCKS, LRU_BLOCK, LRU_BLOCK), LRU_BLOCK ** -0.5),
        'b_rg_a': nrm((N_A_LAYERS, D_RNN), 0.01),
        'w_ig_a': nrm((N_A_LAYERS, N_LRU_BLOCKS, LRU_BLOCK, LRU_BLOCK), LRU_BLOCK ** -0.5),
        'b_ig_a': nrm((N_A_LAYERS, D_RNN), 0.01),
        'lru_lambda_a': lru_lambda_a,
        'w_out_a': nrm((N_A_LAYERS, D_RNN, D_MODEL), D_RNN ** -0.5),
        'g_kv': gain((D_MODEL,)),
        'w_kv': nrm((D_MODEL, 2 * QK_DIM + N_HEADS), D_MODEL ** -0.5),
        'b_f': b_f,
        'g_mix_b': gain((N_B_LAYERS, D_MODEL)),
        'w_q_b': nrm((N_B_LAYERS, D_MODEL, QK_DIM), D_MODEL ** -0.5),
        'w_o_b': nrm((N_B_LAYERS, QK_DIM, D_MODEL), QK_DIM ** -0.5),
        'g_ffn': gain((DEPTH, D_MODEL)),
        'w_gu_dense': nrm((N_DENSE, D_MODEL, 2 * D_FF), D_MODEL ** -0.5),
        'w_down_dense': nrm((N_DENSE, D_FF, D_MODEL), D_FF ** -0.5),
        'w_router': nrm((N_MOE, D_MODEL, N_EXPERTS), D_MODEL ** -0.5),
        'b_router': nrm((N_MOE, N_EXPERTS), 0.01),
        'w_gu_moe': nrm((N_MOE, N_EXPERTS, D_MODEL, 2 * D_FF_EXPERT), D_MODEL ** -0.5),
        'w_down_moe': nrm((N_MOE, N_EXPERTS, D_FF_EXPERT, D_MODEL), D_FF_EXPERT ** -0.5),
        'g_ple': gain((DEPTH, D_MODEL)),
        'w_ple_gate': nrm((DEPTH, D_MODEL, D_MODEL), D_MODEL ** -0.5),
        'w_ple_proj': nrm((DEPTH, P_DIM, D_MODEL), P_DIM ** -0.5),
        'g_final': gain((D_MODEL,)),
    }


def reference(x_prompt, x_sample, state_h, state_conv, cache_k, cache_v, cache_logf, page_table,
              p_prompt, p_sample,
              g_mix_a, w_in_a, conv_w_a, conv_b_a, w_rg_a, b_rg_a, w_ig_a, b_ig_a, lru_lambda_a, w_out_a,
              g_kv, w_kv, b_f, g_mix_b, w_q_b, w_o_b, g_ffn, w_gu_dense, w_down_dense,
              w_router, b_router, w_gu_moe, w_down_moe, g_ple, w_ple_gate, w_ple_proj, g_final):
    w = dict(g_mix_a=g_mix_a, w_in_a=w_in_a, conv_w_a=conv_w_a, conv_b_a=conv_b_a, w_rg_a=w_rg_a,
             b_rg_a=b_rg_a, w_ig_a=w_ig_a, b_ig_a=b_ig_a, lru_lambda_a=lru_lambda_a, w_out_a=w_out_a,
             g_kv=g_kv, w_kv=w_kv, b_f=b_f, g_mix_b=g_mix_b, w_q_b=w_q_b, w_o_b=w_o_b, g_ffn=g_ffn,
             w_gu_dense=w_gu_dense, w_down_dense=w_down_dense, w_router=w_router, b_router=b_router,
             w_gu_moe=w_gu_moe, w_down_moe=w_down_moe, g_ple=g_ple, w_ple_gate=w_ple_gate,
             w_ple_proj=w_ple_proj, g_final=g_final)
    bsz = x_prompt.shape[0]
    h0_prompt = jnp.zeros((N_A_LAYERS, bsz, D_RNN), x_prompt.dtype)
    conv0_prompt = jnp.zeros((N_A_LAYERS, bsz, CONV_W - 1, D_RNN), x_prompt.dtype)
    y_prompt, h_prompt, conv_prompt, k_prompt, v_prompt, logf_prompt = run_group(
        x_prompt, p_prompt, h0_prompt, conv0_prompt, None, w)
    dec_b, n_pages = page_table.shape
    t_past = n_pages * cache_k.shape[1]
    k_past = cache_k[page_table].reshape(dec_b, t_past, N_HEADS, HEAD_DIM)
    v_past = cache_v[page_table].reshape(dec_b, t_past, N_HEADS, HEAD_DIM)
    logf_past = cache_logf[page_table].reshape(dec_b, t_past, N_HEADS)
    y_sample, h_sample, conv_sample, k_sample, v_sample, logf_sample = run_group(
        x_sample, p_sample, state_h, state_conv, (k_past, v_past, logf_past), w)
    return (y_prompt, y_sample, h_prompt, conv_prompt, k_prompt, v_prompt, logf_prompt,
            h_sample, conv_sample, k_sample, v_sample, logf_sample)
```

```python
import functools

import jax
import jax.numpy as jnp
from jax import lax
from jax.experimental import pallas as pl
from jax.experimental.pallas import tpu as pltpu

F32 = jnp.float32
BF16 = jnp.bfloat16

EPS = 1e-6
LRU_C = 8.0
N_LRU_BLOCKS = 8
CONV_W = 4
N_EXPERTS = 8
TOP_K = 2

LANE = 128
SUBLANE = 8
MXU_WIDTH = 256
VMEM_BYTES = 64 * 1024 * 1024
VMEM_LIMIT = VMEM_BYTES * 7 // 8
NEG_BIG = -1e30


def _cparams(*sem):
    return pltpu.CompilerParams(dimension_semantics=sem, vmem_limit_bytes=VMEM_LIMIT)


def _tile(n, pref, align=SUBLANE):
    if n <= pref:
        return n
    t = pref // align * align
    while n % t:
        t -= align
    return t


def _full(shape):
    nd = len(shape)
    return pl.BlockSpec(shape, lambda *_: (0,) * nd)


def _rms(x, g):
    ms = jnp.mean(x * x, axis=-1, keepdims=True)
    return x * lax.rsqrt(ms + EPS) * g


def _dot(a, b):
    return jnp.dot(a, b, preferred_element_type=F32)


def _split3(x):
    hi = x.astype(BF16)
    r1 = x - hi.astype(F32)
    mid = r1.astype(BF16)
    lo = (r1 - mid.astype(F32)).astype(BF16)
    return hi, mid, lo


def _band_windows(r, nblk, tw):
    blk = r // nblk
    wins = []
    c0 = 0
    while c0 < r:
        c1 = min(c0 + tw, r)
        b0, b1 = c0 // blk, (c1 - 1) // blk
        k0 = (b0 * blk) // LANE * LANE
        k1 = min(r, -(-((b1 + 1) * blk) // LANE) * LANE)
        wins.append((c0, c1, k0, k1))
        c0 = c1
    return tuple(wins)


def _band_pack(w, wins, tw):
    n, c, e = w.shape
    r = n * c
    dense = jnp.einsum('nce,nm->ncme', w, jnp.eye(n, dtype=w.dtype)).reshape(r, r)
    parts = []
    for c0, c1, k0, k1 in wins:
        parts.append(jnp.pad(dense[k0:k1, c0:c1], ((0, 0), (0, tw - (c1 - c0)))))
    return jnp.concatenate(parts, axis=0).astype(BF16)


def _band_matmul(x_bf, w_ref, wins):
    outs = []
    off = 0
    for c0, c1, k0, k1 in wins:
        o = _dot(x_bf[:, k0:k1], w_ref[off:off + (k1 - k0), :])
        outs.append(o[:, :c1 - c0])
        off += k1 - k0
    return jnp.concatenate(outs, axis=1) if len(outs) > 1 else outs[0]


def _lru_gates(uc, rg, ig, lam):
    r = jax.nn.sigmoid(rg)
    i = jax.nn.sigmoid(ig)
    log_a = (-LRU_C) * r * jax.nn.softplus(-lam)
    a = jnp.exp(log_a)
    one_minus_a2 = jnp.tanh(-log_a) * (a * a + 1.0)
    return a, jnp.sqrt(one_minus_a2) * (i * uc)


def _rec_prompt_kernel(x_ref, g_ref, win_ref, cw_ref, cb_ref, wrg_ref, brg_ref, wig_ref, big_ref,
                       lam_ref, wout_ref, h_ref, hl_ref, cn_ref, uh_s, a_s, xi_s, hs_s, hc_s,
                       *, wins, tc, r):
    t = pl.program_id(1)
    halo = SUBLANE

    @pl.when(t == 0)
    def _():
        uh_s[0:halo, :] = jnp.zeros((halo, r), F32)
        hc_s[...] = jnp.zeros(hc_s.shape, F32)

    x = x_ref[0]
    xn = _rms(x, g_ref[...]).astype(BF16)
    gu = _dot(xn, win_ref[...])
    gate = gu[:, :r]
    uh_s[halo:halo + tc, :] = gu[:, r:]
    base = halo - (CONV_W - 1)
    uc = cb_ref[...] + uh_s[base:base + tc, :] * cw_ref[0:1, :]
    for j in range(1, CONV_W):
        uc = uc + uh_s[base + j:base + j + tc, :] * cw_ref[j:j + 1, :]

    @pl.when(t == pl.num_programs(1) - 1)
    def _():
        cn_ref[0] = uh_s[halo + tc - (CONV_W - 1):halo + tc, :]

    uh_s[0:halo, :] = uh_s[tc:tc + halo, :]

    uc_bf = uc.astype(BF16)
    rg = _band_matmul(uc_bf, wrg_ref, wins) + brg_ref[...]
    ig = _band_matmul(uc_bf, wig_ref, wins) + big_ref[...]
    a, xin = _lru_gates(uc, rg, ig, lam_ref[...])
    a_s[...] = a
    xi_s[...] = xin

    row = lax.broadcasted_iota(jnp.int32, (SUBLANE, r), 0)

    def body(gi, h_in):
        off = pl.multiple_of(gi * SUBLANE, SUBLANE)
        a8 = a_s[pl.ds(off, SUBLANE), :]
        x8 = xi_s[pl.ds(off, SUBLANE), :]
        for k in (1, 2, 4):
            m = row >= k
            x8 = x8 + a8 * jnp.where(m, pltpu.roll(x8, k, 0), 0.0)
            a8 = a8 * jnp.where(m, pltpu.roll(a8, k, 0), 1.0)
        h8 = a8 * h_in + x8
        hs_s[pl.ds(off, SUBLANE), :] = h8
        return h8[SUBLANE - 1:SUBLANE, :]

    h_last = lax.fori_loop(0, tc // SUBLANE, body, hc_s[0:1, :])
    hc_s[0:1, :] = h_last
    hl_ref[0] = h_last

    y = (hs_s[...] * jax.nn.gelu(gate)).astype(BF16)
    h_ref[0] = x + _dot(y, wout_ref[...])


def _rec_sample_kernel(x_ref, g_ref, win_ref, cw_ref, cb_ref, wrg_ref, brg_ref, wig_ref, big_ref,
                       lam_ref, wout_ref, c0_ref, c1_ref, c2_ref, h0_ref, h_ref, hl_ref, u_ref,
                       *, wins, r):
    x = x_ref[...]
    xn = _rms(x, g_ref[...]).astype(BF16)
    gu = _dot(xn, win_ref[...])
    gate = gu[:, :r]
    u = gu[:, r:]
    uc = cb_ref[...] + c0_ref[...] * cw_ref[0:1, :]
    uc = uc + c1_ref[...] * cw_ref[1:2, :]
    uc = uc + c2_ref[...] * cw_ref[2:3, :]
    uc = uc + u * cw_ref[3:4, :]
    uc_bf = uc.astype(BF16)
    rg = _band_matmul(uc_bf, wrg_ref, wins) + brg_ref[...]
    ig = _band_matmul(uc_bf, wig_ref, wins) + big_ref[...]
    a, xin = _lru_gates(uc, rg, ig, lam_ref[...])
    h = a * h0_ref[...] + xin
    hl_ref[...] = h
    u_ref[...] = u
    y = (h * jax.nn.gelu(gate)).astype(BF16)
    h_ref[...] = x + _dot(y, wout_ref[...])


def _rec_weights(w_in, conv_w, conv_b, w_rg, b_rg, w_ig, b_ig, lam, w_out):
    r = w_out.shape[0]
    tw = min(MXU_WIDTH, r)
    wins = _band_windows(r, w_rg.shape[0], tw)
    args = (w_in.astype(BF16), conv_w, conv_b.reshape(1, r), _band_pack(w_rg, wins, tw), b_rg.reshape(1, r),
            _band_pack(w_ig, wins, tw), b_ig.reshape(1, r), lam.reshape(1, r), w_out.astype(BF16))
    return wins, args


def _rec_prompt(x, g, wins, wargs):
    b, t, d = x.shape
    r = wargs[-1].shape[0]
    tc = _tile(t, 256)
    kern = functools.partial(_rec_prompt_kernel, wins=wins, tc=tc, r=r)
    wspecs = [_full(a.shape) for a in wargs]
    h, hl, cn = pl.pallas_call(
        kern,
        grid=(b, t // tc),
        in_specs=[pl.BlockSpec((1, tc, d), lambda i, j: (i, j, 0)), _full((1, d))] + wspecs,
        out_specs=[pl.BlockSpec((1, tc, d), lambda i, j: (i, j, 0)),
                   pl.BlockSpec((1, 1, r), lambda i, j: (i, 0, 0)),
                   pl.BlockSpec((1, CONV_W - 1, r), lambda i, j: (i, 0, 0))],
        out_shape=[jax.ShapeDtypeStruct((b, t, d), F32),
                   jax.ShapeDtypeStruct((b, 1, r), F32),
                   jax.ShapeDtypeStruct((b, CONV_W - 1, r), F32)],
        scratch_shapes=[pltpu.VMEM((SUBLANE + tc, r), F32), pltpu.VMEM((tc, r), F32),
                        pltpu.VMEM((tc, r), F32), pltpu.VMEM((tc, r), F32), pltpu.VMEM((SUBLANE, r), F32)],
        compiler_params=_cparams("arbitrary", "arbitrary"),
        name="rec_prompt",
    )(x, g.reshape(1, d), *wargs)
    return h, hl.reshape(b, r), cn


def _rec_sample(x, g, wins, wargs, conv0, h0):
    m, d = x.shape
    r = wargs[-1].shape[0]
    kern = functools.partial(_rec_sample_kernel, wins=wins, r=r)
    h, hl, u = pl.pallas_call(
        kern,
        out_shape=[jax.ShapeDtypeStruct((m, d), F32), jax.ShapeDtypeStruct((m, r), F32),
                   jax.ShapeDtypeStruct((m, r), F32)],
        compiler_params=pltpu.CompilerParams(vmem_limit_bytes=VMEM_LIMIT),
        name="rec_sample",
    )(x, g.reshape(1, d), *wargs, conv0[:, 0], conv0[:, 1], conv0[:, 2], h0)
    conv_new = jnp.stack([conv0[:, 1], conv0[:, 2], u], axis=1)
    return h, hl, conv_new


def _ple(h, p_ref, g_ref, wgate_ref, wproj_ref):
    gate = jax.nn.sigmoid(_dot(_rms(h, g_ref[...]).astype(BF16), wgate_ref[...]))
    return h + gate * _dot(p_ref[...].astype(BF16), wproj_ref[...])


def _ffn_dense_kernel(h_ref, g_ref, wg_ref, wu_ref, wd_ref, p_ref, gp_ref, wpg_ref, wpp_ref,
                      o_ref, hn_s, acc_s):
    j = pl.program_id(1)

    @pl.when(j == 0)
    def _():
        hn_s[...] = _rms(h_ref[...], g_ref[...]).astype(BF16)
        acc_s[...] = jnp.zeros(acc_s.shape, F32)

    hn = hn_s[...]
    gate = _dot(hn, wg_ref[...])
    up = _dot(hn, wu_ref[...])
    act = (jax.nn.silu(gate) * up).astype(BF16)
    acc_s[...] += _dot(act, wd_ref[...])

    @pl.when(j == pl.num_programs(1) - 1)
    def _():
        o_ref[...] = _ple(h_ref[...] + acc_s[...], p_ref, gp_ref, wpg_ref, wpp_ref)


def _ffn_dense(h, g, w_gu, w_down, p, g_ple, w_pgate, w_pproj):
    m, d = h.shape
    f = w_down.shape[0]
    pd = p.shape[1]
    tm = _tile(m, 512)
    tf = _tile(f, 512, LANE)
    nf = f // tf
    return pl.pallas_call(
        _ffn_dense_kernel,
        grid=(m // tm, nf),
        in_specs=[pl.BlockSpec((tm, d), lambda i, j: (i, 0)), _full((1, d)),
                  pl.BlockSpec((d, tf), lambda i, j: (0, j)),
                  pl.BlockSpec((d, tf), lambda i, j: (0, nf + j)),
                  pl.BlockSpec((tf, d), lambda i, j: (j, 0)),
                  pl.BlockSpec((tm, pd), lambda i, j: (i, 0)), _full((1, d)),
                  _full((d, d)), _full((pd, d))],
        out_specs=pl.BlockSpec((tm, d), lambda i, j: (i, 0)),
        out_shape=jax.ShapeDtypeStruct((m, d), F32),
        scratch_shapes=[pltpu.VMEM((tm, d), BF16), pltpu.VMEM((tm, d), F32)],
        compiler_params=_cparams("arbitrary", "arbitrary"),
        name="ffn_dense",
    )(h, g.reshape(1, d), w_gu, w_gu, w_down, p, g_ple.reshape(1, d), w_pgate, w_pproj)


def _kvq_kernel(h_ref, gkv_ref, gq_ref, wkv_ref, bf_ref, wq_ref, k_ref, v_ref, lf_ref, q_ref, *rest,
                qk, nh, transposed):
    x = h_ref[...]
    xh = x * lax.rsqrt(jnp.mean(x * x, axis=-1, keepdims=True) + EPS)
    kvf = _dot((xh * gkv_ref[...]).astype(BF16), wkv_ref[...])
    k_ref[...] = kvf[:, :qk]
    v_ref[...] = kvf[:, qk:2 * qk]
    lf = jax.nn.log_sigmoid(kvf[:, 2 * qk:] + bf_ref[...])
    lf_ref[...] = lf[:, :nh]
    if transposed:
        rest[0][...] = lf.T[:nh, :]
    q_ref[...] = _dot((xh * gq_ref[...]).astype(BF16), wq_ref[...]).astype(BF16)


def _kvq(h, g_kv, g_q, w_kv, b_f, w_q, *, transposed):
    m, d = h.shape
    qk = w_q.shape[1]
    nh = b_f.shape[0]
    tm = _tile(m, 512)
    wkv = jnp.pad(w_kv, ((0, 0), (0, LANE - nh))).astype(BF16)
    bfp = jnp.pad(b_f, (0, LANE - nh)).reshape(1, LANE)
    out_specs = [pl.BlockSpec((tm, qk), lambda i: (i, 0)), pl.BlockSpec((tm, qk), lambda i: (i, 0)),
                 pl.BlockSpec((tm, nh), lambda i: (i, 0)), pl.BlockSpec((tm, qk), lambda i: (i, 0))]
    out_shape = [jax.ShapeDtypeStruct((m, qk), F32), jax.ShapeDtypeStruct((m, qk), F32),
                 jax.ShapeDtypeStruct((m, nh), F32), jax.ShapeDtypeStruct((m, qk), BF16)]
    if transposed:
        out_specs.append(pl.BlockSpec((nh, tm), lambda i: (0, i)))
        out_shape.append(jax.ShapeDtypeStruct((nh, m), F32))
    kern = functools.partial(_kvq_kernel, qk=qk, nh=nh, transposed=transposed)
    return pl.pallas_call(
        kern,
        grid=(m // tm,),
        in_specs=[pl.BlockSpec((tm, d), lambda i: (i, 0)), _full((1, d)), _full((1, d)),
                  _full(wkv.shape), _full((1, LANE)), _full((d, qk))],
        out_specs=out_specs,
        out_shape=out_shape,
        compiler_params=_cparams("arbitrary"),
        name="kvq",
    )(h, g_kv.reshape(1, d), g_q.reshape(1, d), wkv, bfp, w_q.astype(BF16))


def _cumsum_lanes(x):
    n = x.shape[1]
    lane = lax.broadcasted_iota(jnp.int32, x.shape, 1)
    k = 1
    while k < n:
        x = x + jnp.where(lane >= k, pltpu.roll(x, k, 1), 0.0)
        k *= 2
    return x


def _attn_prompt_kernel(q_ref, k_ref, v_ref, lft_ref, o_ref, kb_s, vb_s, ck_s, *, tq, scale):
    h = pl.program_id(1)
    iq = pl.program_id(2)
    nblk = ck_s.shape[0]

    @pl.when(iq == 0)
    def _():
        kb_s[...] = k_ref[0].astype(BF16)
        vb_s[...] = v_ref[0].astype(BF16)

    @pl.when((iq == 0) & (h == 0))
    def _():
        c = _cumsum_lanes(lft_ref[...])
        for j in range(nblk):
            ck_s[j] = c[:, j * tq:(j + 1) * tq]

    q = q_ref[0]
    hd = q.shape[1]

    def step(kb, carry, masked):
        m, l, acc = carry
        ks = pl.multiple_of(kb * tq, tq)
        s = lax.dot_general(q, kb_s[pl.ds(ks, tq), :], (((1,), (1,)), ((), ())),
                            preferred_element_type=F32) * scale
        s = s - ck_s[kb, pl.ds(h, 1), :]
        if masked:
            row = lax.broadcasted_iota(jnp.int32, (tq, tq), 0)
            col = lax.broadcasted_iota(jnp.int32, (tq, tq), 1)
            s = jnp.where(col <= row, s, NEG_BIG)
        m_new = jnp.maximum(m, jnp.max(s, axis=1, keepdims=True))
        alpha = jnp.exp(m - m_new)
        p = jnp.exp(s - m_new)
        l = alpha * l + jnp.sum(p, axis=1, keepdims=True)
        acc = alpha * acc + _dot(p.astype(BF16), vb_s[pl.ds(ks, tq), :])
        return m_new, l, acc

    init = (jnp.full((tq, 1), NEG_BIG, F32), jnp.zeros((tq, 1), F32), jnp.zeros((tq, hd), F32))
    carry = lax.fori_loop(0, iq, lambda kb, c: step(kb, c, False), init)
    m, l, acc = step(iq, carry, True)
    o_ref[0] = (acc / l).astype(BF16)


def _attn_prompt(q, k, v, lft, *, nh, scale):
    b, t, qk = q.shape
    hd = qk // nh
    tq = _tile(t, 256, LANE)
    nq = t // tq
    kern = functools.partial(_attn_prompt_kernel, tq=tq, scale=scale)
    return pl.pallas_call(
        kern,
        grid=(b, nh, nq),
        in_specs=[pl.BlockSpec((1, tq, hd), lambda i, h, j: (i, j, h)),
                  pl.BlockSpec((1, t, hd), lambda i, h, j: (i, 0, h)),
                  pl.BlockSpec((1, t, hd), lambda i, h, j: (i, 0, h)),
                  pl.BlockSpec((nh, t), lambda i, h, j: (0, i))],
        out_specs=pl.BlockSpec((1, tq, hd), lambda i, h, j: (i, j, h)),
        out_shape=jax.ShapeDtypeStruct((b, t, qk), BF16),
        scratch_shapes=[pltpu.VMEM((t, hd), BF16), pltpu.VMEM((t, hd), BF16),
                        pltpu.VMEM((nq, nh, tq), F32)],
        compiler_params=_cparams("arbitrary", "arbitrary", "arbitrary"),
        name="attn_prompt",
    )(q, k, v, lft)


def _attn_sample_kernel(pt_ref, q_ref, kn_ref, vn_ref, lfn_ref, expand_ref, upper_ref, *rest,
                        pp, nh, hd, scale):
    k_refs = rest[0:pp]
    v_refs = rest[pp:2 * pp]
    lf_refs = rest[2 * pp:3 * pp]
    o_ref = rest[3 * pp]
    qbd_s, pad_s, m_s, l_s, acc_s, carry_s = rest[3 * pp + 1:]
    del pt_ref
    p = pl.program_id(1)
    qk = nh * hd
    lane = lax.broadcasted_iota(jnp.int32, (LANE, LANE), 1)
    expand = expand_ref[...]

    def expand_exact(x):
        hi, mid, lo = _split3(x)
        return _dot(hi, expand) + _dot(mid, expand) + _dot(lo, expand)

    @pl.when(p == 0)
    def _():
        pad_s[...] = jnp.zeros(pad_s.shape, F32)
        pad_s[0:nh, :] = q_ref[0]
        qt = pad_s[...].T
        for h in range(nh):
            qbd_s[h * hd:(h + 1) * hd, :] = jnp.where(lane == h, qt, 0.0).astype(BF16)
        kn = jnp.broadcast_to(kn_ref[0], (SUBLANE, qk)).astype(BF16)
        m_s[...] = (_dot(kn, qbd_s[...]) * scale)[0:1, :]
        l_s[...] = jnp.ones(l_s.shape, F32)
        row = lax.broadcasted_iota(jnp.int32, (SUBLANE, qk), 0)
        acc_s[...] = jnp.where(row == 0, jnp.broadcast_to(vn_ref[0], (SUBLANE, qk)), 0.0)
        carry_s[...] = lfn_ref[0]
        pad_s[...] = jnp.zeros(pad_s.shape, F32)

    qbd = qbd_s[...]
    upper = upper_ref[...]
    for j in range(pp):
        pad_s[:, 0:nh] = lf_refs[j][0]
        lf = pad_s[...]
        hi, mid, lo = _split3(lf)
        bias = _dot(upper, hi) + _dot(upper, mid) + _dot(upper, lo) + carry_s[...]
        carry_s[...] = carry_s[...] + jnp.sum(lf, axis=0, keepdims=True)
        s = _dot(k_refs[j][0].astype(BF16), qbd) * scale + bias
        m_old = m_s[...]
        m_new = jnp.maximum(m_old, jnp.max(s, axis=0, keepdims=True))
        alpha = jnp.exp(m_old - m_new)
        pr = jnp.exp(s - m_new)
        m_s[...] = m_new
        l_s[...] = alpha * l_s[...] + jnp.sum(pr, axis=0, keepdims=True)
        pe = _dot(pr.astype(BF16), expand)
        pv = pe * v_refs[j][0]
        part = jnp.sum(pv.reshape(pv.shape[0] // SUBLANE, SUBLANE, qk), axis=0)
        alpha_e = expand_exact(jnp.broadcast_to(alpha, (SUBLANE, LANE)))
        acc_s[...] = alpha_e * acc_s[...] + part

    @pl.when(p == pl.num_programs(1) - 1)
    def _():
        l_e = expand_exact(jnp.broadcast_to(l_s[...], (SUBLANE, LANE)))
        o_ref[0] = jnp.sum(acc_s[...], axis=0, keepdims=True) / l_e[0:1, :]


def _attn_sample(q, k_new, v_new, lf_new, cache_k, cache_v, cache_logf, page_table, *, scale):
    b, qk = q.shape
    n_phys, ps, nh, hd = cache_k.shape
    assert ps == LANE and hd == LANE
    n_pages = page_table.shape[1]
    pp = 4
    while n_pages % pp:
        pp //= 2
    ck = cache_k.reshape(n_phys, ps, qk)
    cv = cache_v.reshape(n_phys, ps, qk)
    pt = page_table.reshape(-1)
    head_of_lane = jnp.arange(qk) // hd
    expand = (jnp.arange(LANE)[:, None] == head_of_lane[None, :]).astype(BF16)
    upper = (jnp.arange(ps)[None, :] > jnp.arange(ps)[:, None]).astype(BF16)
    lfn = jnp.pad(lf_new, ((0, 0), (0, LANE - nh))).reshape(b, 1, LANE)

    def page_map(j):
        return lambda i, p, pt_ref: (pt_ref[i * n_pages + (n_pages - 1 - (p * pp + j))], 0, 0)

    row_spec = lambda w: pl.BlockSpec((1, 1, w), lambda i, p, pt_ref: (i, 0, 0))
    const_spec = lambda s: pl.BlockSpec(s, lambda i, p, pt_ref: (0, 0))
    in_specs = [pl.BlockSpec((1, nh, hd), lambda i, p, pt_ref: (i, 0, 0)), row_spec(qk), row_spec(qk),
                row_spec(LANE), const_spec((LANE, qk)), const_spec((ps, ps))]
    in_specs += [pl.BlockSpec((1, ps, qk), page_map(j)) for j in range(pp)]
    in_specs += [pl.BlockSpec((1, ps, qk), page_map(j)) for j in range(pp)]
    in_specs += [pl.BlockSpec((1, ps, nh), page_map(j)) for j in range(pp)]
    kern = functools.partial(_attn_sample_kernel, pp=pp, nh=nh, hd=hd, scale=scale)
    out = pl.pallas_call(
        kern,
        grid_spec=pltpu.PrefetchScalarGridSpec(
            num_scalar_prefetch=1,
            grid=(b, n_pages // pp),
            in_specs=in_specs,
            out_specs=pl.BlockSpec((1, 1, qk), lambda i, p, pt_ref: (i, 0, 0)),
            scratch_shapes=[pltpu.VMEM((qk, LANE), BF16), pltpu.VMEM((ps, LANE), F32),
                            pltpu.VMEM((1, LANE), F32), pltpu.VMEM((1, LANE), F32),
                            pltpu.VMEM((SUBLANE, qk), F32), pltpu.VMEM((1, LANE), F32)]),
        out_shape=jax.ShapeDtypeStruct((b, 1, qk), F32),
        compiler_params=_cparams("arbitrary", "arbitrary"),
        name="attn_sample",
    )(pt, q.reshape(b, nh, hd), k_new.reshape(b, 1, qk), v_new.reshape(b, 1, qk), lfn, expand, upper,
      *([ck] * pp), *([cv] * pp), *([cache_logf] * pp))
    return out.reshape(b, qk)


def _oproj_router_kernel(a_ref, h_ref, wo_ref, g_ref, wr_ref, br_ref, h4_ref, hn_ref, route_ref, cnt_ref,
                         carry_s, *, n_exp):
    i = pl.program_id(0)

    @pl.when(i == 0)
    def _():
        carry_s[...] = jnp.zeros(carry_s.shape, F32)

    h4 = h_ref[...] + _dot(a_ref[...].astype(BF16), wo_ref[...])
    h4_ref[...] = h4
    hn = _rms(h4, g_ref[...])
    hn_ref[...] = hn
    tm = hn.shape[0]

    x_hi, x_mid, _ = _split3(hn)
    w_hi, w_mid, _ = _split3(wr_ref[...])
    logits = _dot(x_hi, w_hi) + _dot(x_mid, w_hi) + _dot(x_hi, w_mid) + br_ref[...]

    lane = lax.broadcasted_iota(jnp.int32, (tm, LANE), 1)
    valid = lane < n_exp
    logits = jnp.where(valid, logits, NEG_BIG)
    ex = jnp.exp(logits - jnp.max(logits, axis=-1, keepdims=True))
    probs = jnp.where(valid, ex / jnp.sum(ex, axis=-1, keepdims=True), -1.0)
    p1 = jnp.max(probs, axis=-1, keepdims=True)
    i1 = jnp.min(jnp.where(probs == p1, lane, LANE), axis=-1, keepdims=True)
    probs2 = jnp.where(lane == i1, -1.0, probs)
    p2 = jnp.max(probs2, axis=-1, keepdims=True)
    i2 = jnp.min(jnp.where(probs2 == p2, lane, LANE), axis=-1, keepdims=True)
    den = p1 + p2
    w1 = p1 / den
    w2 = p2 / den

    onehot = ((lane == i1) | (lane == i2)).astype(F32)
    r_i = lax.broadcasted_iota(jnp.int32, (tm, tm), 0)
    c_i = lax.broadcasted_iota(jnp.int32, (tm, tm), 1)
    strict_lower = (c_i < r_i).astype(BF16)
    ranks = _dot(strict_lower, onehot.astype(BF16)) + carry_s[0:1, :]
    r1 = jnp.sum(jnp.where(lane == i1, ranks, 0.0), axis=-1, keepdims=True)
    r2 = jnp.sum(jnp.where(lane == i2, ranks, 0.0), axis=-1, keepdims=True)
    carry_s[0:1, :] = carry_s[0:1, :] + jnp.sum(onehot, axis=0, keepdims=True)
    cnt_ref[...] = jnp.broadcast_to(carry_s[0:1, :], cnt_ref.shape)

    route = jnp.where(lane == 0, i1.astype(F32), 0.0)
    route = jnp.where(lane == 1, i2.astype(F32), route)
    route = jnp.where(lane == 2, w1, route)
    route = jnp.where(lane == 3, w2, route)
    route = jnp.where(lane == 4, r1, route)
    route = jnp.where(lane == 5, r2, route)
    route_ref[...] = route


def _oproj_router(attn, h, w_o, g, w_router, b_router):
    m, d = h.shape
    qk = attn.shape[1]
    n_exp = w_router.shape[1]
    tm = _tile(m, 512)
    wr = jnp.pad(w_router, ((0, 0), (0, LANE - n_exp)))
    br = jnp.pad(b_router, (0, LANE - n_exp)).reshape(1, LANE)
    kern = functools.partial(_oproj_router_kernel, n_exp=n_exp)
    row = lambda w: pl.BlockSpec((tm, w), lambda i: (i, 0))
    return pl.pallas_call(
        kern,
        grid=(m // tm,),
        in_specs=[row(qk), row(d), _full((qk, d)), _full((1, d)), _full((d, LANE)), _full((1, LANE))],
        out_specs=[row(d), row(d), row(LANE), _full((SUBLANE, LANE))],
        out_shape=[jax.ShapeDtypeStruct((m, d), F32), jax.ShapeDtypeStruct((m, d), F32),
                   jax.ShapeDtypeStruct((m, LANE), F32), jax.ShapeDtypeStruct((SUBLANE, LANE), F32)],
        scratch_shapes=[pltpu.VMEM((SUBLANE, LANE), F32)],
        compiler_params=_cparams("arbitrary"),
        name="oproj_router",
    )(attn, h, w_o.astype(BF16), g.reshape(1, d), wr, br)


def _row_copy_wait(src, dst, sem, rows):
    pltpu.make_async_copy(src.at[pl.ds(0, rows)], dst.at[pl.ds(0, rows)], sem).wait()


def _dispatch_kernel(pos_ref, hn_ref, xs_in_ref, xs_ref, sem, *, tm):
    del xs_in_ref
    i = pl.program_id(0)

    def body(r, c):
        for k in range(TOP_K):
            dst = pos_ref[(i * tm + r) * TOP_K + k]
            pltpu.make_async_copy(hn_ref.at[pl.ds(r, 1)], xs_ref.at[pl.ds(dst, 1)], sem).start()
        return c

    lax.fori_loop(0, tm, body, 0)
    for _ in range(TOP_K):
        _row_copy_wait(hn_ref, xs_ref, sem, tm)


def _dispatch(hn, pos, n_sorted):
    m, d = hn.shape
    tm = _tile(m, 512)
    kern = functools.partial(_dispatch_kernel, tm=tm)
    return pl.pallas_call(
        kern,
        grid_spec=pltpu.PrefetchScalarGridSpec(
            num_scalar_prefetch=1,
            grid=(m // tm,),
            in_specs=[pl.BlockSpec((tm, d), lambda i, pos_ref: (i, 0)),
                      pl.BlockSpec(memory_space=pl.ANY)],
            out_specs=pl.BlockSpec(memory_space=pl.ANY),
            scratch_shapes=[pltpu.SemaphoreType.DMA(())]),
        out_shape=jax.ShapeDtypeStruct((n_sorted, d), F32),
        input_output_aliases={2: 0},
        compiler_params=_cparams("arbitrary"),
        name="moe_dispatch",
    )(pos, hn, jnp.zeros((n_sorted, d), F32))


def _experts_kernel(te_ref, tv_ref, x_ref, wg_ref, wu_ref, wd_ref, y_ref, xb_s, acc_s):
    del te_ref
    i = pl.program_id(0)
    j = pl.program_id(1)
    valid = tv_ref[i] > 0

    @pl.when(valid & (j == 0))
    def _():
        xb_s[...] = x_ref[...].astype(BF16)
        acc_s[...] = jnp.zeros(acc_s.shape, F32)

    @pl.when(valid)
    def _():
        x = xb_s[...]
        gate = _dot(x, wg_ref[0].astype(BF16))
        up = _dot(x, wu_ref[0].astype(BF16))
        act = (jax.nn.silu(gate) * up).astype(BF16)
        acc_s[...] += _dot(act, wd_ref[0].astype(BF16))

    last = j == pl.num_programs(1) - 1

    @pl.when(valid & last)
    def _():
        y_ref[...] = acc_s[...]

    @pl.when(jnp.logical_not(valid) & last)
    def _():
        y_ref[...] = jnp.zeros(y_ref.shape, F32)


def _experts(xs, tile_expert, tile_valid, w_gu, w_down, tm):
    ns, d = xs.shape
    f = w_down.shape[1]
    tf = _tile(f, 512, LANE)
    nf = f // tf
    nt = ns // tm

    def jj(i, j, tv):
        return jnp.where(tv[i] > 0, j, nf - 1)

    return pl.pallas_call(
        _experts_kernel,
        grid_spec=pltpu.PrefetchScalarGridSpec(
            num_scalar_prefetch=2,
            grid=(nt, nf),
            in_specs=[pl.BlockSpec((tm, d), lambda i, j, te, tv: (i, 0)),
                      pl.BlockSpec((1, d, tf), lambda i, j, te, tv: (te[i], 0, jj(i, j, tv))),
                      pl.BlockSpec((1, d, tf), lambda i, j, te, tv: (te[i], 0, nf + jj(i, j, tv))),
                      pl.BlockSpec((1, tf, d), lambda i, j, te, tv: (te[i], jj(i, j, tv), 0))],
            out_specs=pl.BlockSpec((tm, d), lambda i, j, te, tv: (i, 0)),
            scratch_shapes=[pltpu.VMEM((tm, d), BF16), pltpu.VMEM((tm, d), F32)]),
        out_shape=jax.ShapeDtypeStruct((ns, d), F32),
        compiler_params=_cparams("arbitrary", "arbitrary"),
        name="moe_experts",
    )(tile_expert, tile_valid, xs, w_gu, w_gu, w_down)


def _combine_kernel(pos_ref, ys_ref, h_ref, route_ref, p_ref, gp_ref, wpg_ref, wpp_ref, gf_ref, o_ref,
                    buf_s, sem, *, tm):
    i = pl.program_id(0)
    n = pl.num_programs(0)

    def issue(step, slot):
        def body(r, c):
            for k in range(TOP_K):
                src = pos_ref[(step * tm + r) * TOP_K + k]
                pltpu.make_async_copy(ys_ref.at[pl.ds(src, 1)], buf_s.at[slot, k, pl.ds(r, 1)],
                                      sem.at[slot]).start()
            return c
        lax.fori_loop(0, tm, body, 0)

    @pl.when(i == 0)
    def _():
        issue(0, 0)

    slot = lax.rem(i, 2)

    @pl.when(i + 1 < n)
    def _():
        issue(i + 1, 1 - slot)

    for k in range(TOP_K):
        _row_copy_wait(ys_ref, buf_s.at[slot, k], sem.at[slot], tm)

    route = route_ref[...]
    h = h_ref[...] + route[:, 2:3] * buf_s[slot, 0] + route[:, 3:4] * buf_s[slot, 1]
    h = _ple(h, p_ref, gp_ref, wpg_ref, wpp_ref)
    o_ref[...] = _rms(h, gf_ref[...])


def _combine(ys, pos, h, route, p, g_ple, w_pgate, w_pproj, g_final):
    m, d = h.shape
    pd = p.shape[1]
    tm = _tile(m, 256)
    kern = functools.partial(_combine_kernel, tm=tm)
    row = lambda w: pl.BlockSpec((tm, w), lambda i, pos_ref: (i, 0))
    const = lambda s: pl.BlockSpec(s, lambda i, pos_ref: (0, 0))
    return pl.pallas_call(
        kern,
        grid_spec=pltpu.PrefetchScalarGridSpec(
            num_scalar_prefetch=1,
            grid=(m // tm,),
            in_specs=[pl.BlockSpec(memory_space=pl.ANY), row(d), row(LANE), row(pd), const((1, d)),
                      const((d, d)), const((pd, d)), const((1, d))],
            out_specs=row(d),
            scratch_shapes=[pltpu.VMEM((2, TOP_K, tm, d), F32), pltpu.SemaphoreType.DMA((2,))]),
        out_shape=jax.ShapeDtypeStruct((m, d), F32),
        compiler_params=_cparams("arbitrary"),
        name="moe_combine",
    )(pos, ys, h, route, p, g_ple.reshape(1, d), w_pgate, w_pproj, g_final.reshape(1, d))


def _route_plan(route, counts, n_exp, tm, n_tiles):
    e = route[:, 0:TOP_K].astype(jnp.int32)
    rank = route[:, 4:4 + TOP_K].astype(jnp.int32)
    cnt = counts[0, :n_exp].astype(jnp.int32)
    ntile = (cnt + tm - 1) // tm
    tile_end = jnp.cumsum(ntile)
    row_start = (tile_end - ntile) * tm
    sel = e[:, :, None] == jnp.arange(n_exp)[None, None, :]
    pos = jnp.sum(jnp.where(sel, row_start[None, None, :], 0), axis=-1) + rank
    tiles = jnp.arange(n_tiles)
    total = tile_end[-1]
    te = jnp.minimum(jnp.sum(tiles[:, None] >= tile_end[None, :], axis=1), n_exp - 1)
    valid = tiles < total
    te_last = jnp.sum(jnp.where(tiles == total - 1, te, 0))
    te = jnp.where(valid, te, te_last).astype(jnp.int32)
    return pos.reshape(-1).astype(jnp.int32), te, valid.astype(jnp.int32)


def _moe_and_final(attn, h, w_o, g_ffn, w_router, b_router, w_gu, w_down, p, g_ple, w_pgate, w_pproj,
                   g_final, *, tm_expert):
    m, d = h.shape
    n_exp = w_router.shape[1]
    h4, hn, route, counts = _oproj_router(attn, h, w_o, g_ffn, w_router, b_router)
    n_tiles = -(-(m * TOP_K) // tm_expert) + n_exp
    pos, te, tv = _route_plan(route, counts, n_exp, tm_expert, n_tiles)
    xs = _dispatch(hn, pos, n_tiles * tm_expert)
    ys = _experts(xs, te, tv, w_gu, w_down, tm_expert)
    return _combine(ys, pos, h4, route, p, g_ple, w_pgate, w_pproj, g_final)


def kernel(x_prompt, x_sample, state_h, state_conv, cache_k, cache_v, cache_logf, page_table, p_prompt, p_sample, g_mix_a, w_in_a, conv_w_a, conv_b_a, w_rg_a, b_rg_a, w_ig_a, b_ig_a, lru_lambda_a, w_out_a, g_kv, w_kv, b_f, g_mix_b, w_q_b, w_o_b, g_ffn, w_gu_dense, w_down_dense, w_router, b_router, w_gu_moe, w_down_moe, g_ple, w_ple_gate, w_ple_proj, g_final):
    assert w_in_a.shape[0] == 1 and w_q_b.shape[0] == 1 and g_ffn.shape[0] == 2
    assert x_sample.shape[1] == 1
    bsz, t_len, d = x_prompt.shape
    dec_b = x_sample.shape[0]
    nh = b_f.shape[0]
    qk = w_q_b.shape[2]
    hd = qk // nh
    r = w_out_a.shape[1]
    scale = hd ** -0.5
    mp = bsz * t_len

    wins, rec_w = _rec_weights(w_in_a[0], conv_w_a[0], conv_b_a[0], w_rg_a[0], b_rg_a[0], w_ig_a[0],
                               b_ig_a[0], lru_lambda_a[0], w_out_a[0])
    w_gu_d = w_gu_dense[0].astype(BF16)
    w_down_d = w_down_dense[0].astype(BF16)
    w_pg = [w_ple_gate[l].astype(BF16) for l in range(2)]
    w_pp = [w_ple_proj[l].astype(BF16) for l in range(2)]

    def rest_of_step(h1, p, transposed):
        h3 = _ffn_dense(h1, g_ffn[0], w_gu_d, w_down_d, p[0], g_ple[0], w_pg[0], w_pp[0])
        return h3, _kvq(h3, g_kv, g_mix_b[0], w_kv, b_f, w_q_b[0], transposed=transposed)

    def moe(attn, h3, p, tm_expert):
        return _moe_and_final(attn, h3, w_o_b[0], g_ffn[1], w_router[0], b_router[0], w_gu_moe[0],
                              w_down_moe[0], p[1], g_ple[1], w_pg[1], w_pp[1], g_final, tm_expert=tm_expert)

    h1, hl_p, cn_p = _rec_prompt(x_prompt, g_mix_a[0], wins, rec_w)
    pp = p_prompt.reshape(2, mp, -1)
    h3, (k_p, v_p, lf_p, q_p, lft_p) = rest_of_step(h1.reshape(mp, d), pp, True)
    attn_p = _attn_prompt(q_p.reshape(bsz, t_len, qk), k_p.reshape(bsz, t_len, qk), v_p.reshape(bsz, t_len, qk),
                          lft_p, nh=nh, scale=scale)
    y_p = moe(attn_p.reshape(mp, qk), h3, pp, min(1024, mp))

    h1s, hl_s, cn_s = _rec_sample(x_sample.reshape(dec_b, d), g_mix_a[0], wins, rec_w, state_conv[0], state_h[0])
    ps = p_sample.reshape(2, dec_b, -1)
    h3s, (k_s, v_s, lf_s, q_s) = rest_of_step(h1s, ps, False)
    attn_s = _attn_sample(q_s.astype(F32), k_s, v_s, lf_s, cache_k, cache_v, cache_logf, page_table, scale=scale)
    y_s = moe(attn_s, h3s, ps, dec_b)

    return (y_p.reshape(bsz, t_len, d), y_s.reshape(dec_b, 1, d),
            hl_p[None], cn_p[None],
            k_p.reshape(bsz, t_len, nh, hd), v_p.reshape(bsz, t_len, nh, hd), lf_p.reshape(bsz, t_len, nh),
            hl_s[None], cn_s[None],
            k_s.reshape(dec_b, 1, nh, hd), v_s.reshape(dec_b, 1, nh, hd), lf_s.reshape(dec_b, 1, nh))
```

```python
import functools

import jax
import jax.numpy as jnp
from jax import lax
from jax.experimental import pallas as pl
from jax.experimental.pallas import tpu as pltpu

F32 = jnp.float32
BF16 = jnp.bfloat16

EPS = 1e-6
LRU_C = 8.0
N_LRU_BLOCKS = 8
CONV_W = 4
N_EXPERTS = 8
TOP_K = 2

LANE = 128
SUBLANE = 8
MXU_WIDTH = 256
VMEM_BYTES = 64 * 1024 * 1024
VMEM_LIMIT = VMEM_BYTES * 7 // 8
NEG_BIG = -1e30
LOG2E = 1.4426950408889634


def _cparams(*sem):
    return pltpu.CompilerParams(dimension_semantics=sem, vmem_limit_bytes=VMEM_LIMIT)


def _tile(n, pref, align=SUBLANE):
    if n <= pref:
        return n
    t = pref // align * align
    while n % t:
        t -= align
    return t


def _full(shape):
    nd = len(shape)
    return pl.BlockSpec(shape, lambda *_: (0,) * nd)


def _rms(x, g):
    ms = jnp.mean(x * x, axis=-1, keepdims=True)
    return x * lax.rsqrt(ms + EPS) * g


def _dot(a, b):
    return jnp.dot(a, b, preferred_element_type=F32)


def _split3_f32(x):
    hi = x.astype(BF16).astype(F32)
    r1 = x - hi
    mid = r1.astype(BF16).astype(F32)
    lo = (r1 - mid).astype(BF16).astype(F32)
    return hi, mid, lo


def _split3(x):
    return tuple(piece.astype(BF16) for piece in _split3_f32(x))


def _band_windows(r, nblk, tw):
    blk = r // nblk
    wins = []
    c0 = 0
    while c0 < r:
        c1 = min(c0 + tw, r)
        b0, b1 = c0 // blk, (c1 - 1) // blk
        k0 = (b0 * blk) // LANE * LANE
        k1 = min(r, -(-((b1 + 1) * blk) // LANE) * LANE)
        wins.append((c0, c1, k0, k1))
        c0 = c1
    return tuple(wins)


def _band_pack(w, wins, tw):
    n, c, e = w.shape
    r = n * c
    dense = jnp.einsum('nce,nm->ncme', w, jnp.eye(n, dtype=w.dtype)).reshape(r, r)
    parts = []
    for c0, c1, k0, k1 in wins:
        parts.append(jnp.pad(dense[k0:k1, c0:c1], ((0, 0), (0, tw - (c1 - c0)))))
    return jnp.concatenate(parts, axis=0).astype(BF16)


def _band_matmul(x_bf, w_ref, wins):
    outs = []
    off = 0
    for c0, c1, k0, k1 in wins:
        o = _dot(x_bf[:, k0:k1], w_ref[off:off + (k1 - k0), :])
        outs.append(o[:, :c1 - c0])
        off += k1 - k0
    return jnp.concatenate(outs, axis=1) if len(outs) > 1 else outs[0]


def _lru_gates(uc, rg, ig, lam):
    r = jax.nn.sigmoid(rg)
    i = jax.nn.sigmoid(ig)
    log_a = (-LRU_C) * r * jax.nn.softplus(-lam)
    a = jnp.exp(log_a)
    one_minus_a2 = jnp.tanh(-log_a) * (a * a + 1.0)
    return a, jnp.sqrt(one_minus_a2) * (i * uc)


def _rec_prompt_kernel(x_ref, g_ref, win_ref, cw_ref, cb_ref, wrg_ref, brg_ref, wig_ref, big_ref,
                       lam_ref, wout_ref, h_ref, hl_ref, cn_ref, uh_s, a_s, xi_s, hs_s, hc_s,
                       *, wins, tc, r):
    t = pl.program_id(1)
    halo = SUBLANE

    @pl.when(t == 0)
    def _():
        uh_s[0:halo, :] = jnp.zeros((halo, r), F32)
        hc_s[...] = jnp.zeros(hc_s.shape, F32)

    x = x_ref[0]
    xn = _rms(x, g_ref[...]).astype(BF16)
    gu = _dot(xn, win_ref[...])
    gate = gu[:, :r]
    uh_s[halo:halo + tc, :] = gu[:, r:]
    base = halo - (CONV_W - 1)
    uc = cb_ref[...] + uh_s[base:base + tc, :] * cw_ref[0:1, :]
    for j in range(1, CONV_W):
        uc = uc + uh_s[base + j:base + j + tc, :] * cw_ref[j:j + 1, :]

    @pl.when(t == pl.num_programs(1) - 1)
    def _():
        cn_ref[0] = uh_s[halo + tc - (CONV_W - 1):halo + tc, :]

    uh_s[0:halo, :] = uh_s[tc:tc + halo, :]

    uc_bf = uc.astype(BF16)
    rg = _band_matmul(uc_bf, wrg_ref, wins) + brg_ref[...]
    ig = _band_matmul(uc_bf, wig_ref, wins) + big_ref[...]
    a, xin = _lru_gates(uc, rg, ig, lam_ref[...])
    a_s[...] = a
    xi_s[...] = xin

    row = lax.broadcasted_iota(jnp.int32, (SUBLANE, r), 0)

    def body(gi, h_in):
        off = pl.multiple_of(gi * SUBLANE, SUBLANE)
        a8 = a_s[pl.ds(off, SUBLANE), :]
        x8 = xi_s[pl.ds(off, SUBLANE), :]
        for k in (1, 2, 4):
            m = row >= k
            x8 = x8 + a8 * jnp.where(m, pltpu.roll(x8, k, 0), 0.0)
            a8 = a8 * jnp.where(m, pltpu.roll(a8, k, 0), 1.0)
        h8 = a8 * h_in + x8
        hs_s[pl.ds(off, SUBLANE), :] = h8
        return h8[SUBLANE - 1:SUBLANE, :]

    h_last = lax.fori_loop(0, tc // SUBLANE, body, hc_s[0:1, :])
    hc_s[0:1, :] = h_last
    hl_ref[0] = h_last

    y = (hs_s[...] * jax.nn.gelu(gate)).astype(BF16)
    h_ref[0] = x + _dot(y, wout_ref[...])


def _rec_sample_kernel(x_ref, g_ref, win_ref, cw_ref, cb_ref, wrg_ref, brg_ref, wig_ref, big_ref,
                       lam_ref, wout_ref, c0_ref, c1_ref, c2_ref, h0_ref, h_ref, hl_ref, u_ref,
                       *, wins, r):
    x = x_ref[...]
    xn = _rms(x, g_ref[...]).astype(BF16)
    gu = _dot(xn, win_ref[...])
    gate = gu[:, :r]
    u = gu[:, r:]
    uc = cb_ref[...] + c0_ref[...] * cw_ref[0:1, :]
    uc = uc + c1_ref[...] * cw_ref[1:2, :]
    uc = uc + c2_ref[...] * cw_ref[2:3, :]
    uc = uc + u * cw_ref[3:4, :]
    uc_bf = uc.astype(BF16)
    rg = _band_matmul(uc_bf, wrg_ref, wins) + brg_ref[...]
    ig = _band_matmul(uc_bf, wig_ref, wins) + big_ref[...]
    a, xin = _lru_gates(uc, rg, ig, lam_ref[...])
    h = a * h0_ref[...] + xin
    hl_ref[...] = h
    u_ref[...] = u
    y = (h * jax.nn.gelu(gate)).astype(BF16)
    h_ref[...] = x + _dot(y, wout_ref[...])


def _rec_weights(w_in, conv_w, conv_b, w_rg, b_rg, w_ig, b_ig, lam, w_out):
    r = w_out.shape[0]
    tw = min(MXU_WIDTH, r)
    wins = _band_windows(r, w_rg.shape[0], tw)
    args = (w_in.astype(BF16), conv_w, conv_b.reshape(1, r), _band_pack(w_rg, wins, tw), b_rg.reshape(1, r),
            _band_pack(w_ig, wins, tw), b_ig.reshape(1, r), lam.reshape(1, r), w_out.astype(BF16))
    return wins, args


def _rec_prompt(x, g, wins, wargs):
    b, t, d = x.shape
    r = wargs[-1].shape[0]
    tc = _tile(t, 256)
    kern = functools.partial(_rec_prompt_kernel, wins=wins, tc=tc, r=r)
    wspecs = [_full(a.shape) for a in wargs]
    h, hl, cn = pl.pallas_call(
        kern,
        grid=(b, t // tc),
        in_specs=[pl.BlockSpec((1, tc, d), lambda i, j: (i, j, 0)), _full((1, d))] + wspecs,
        out_specs=[pl.BlockSpec((1, tc, d), lambda i, j: (i, j, 0)),
                   pl.BlockSpec((1, 1, r), lambda i, j: (i, 0, 0)),
                   pl.BlockSpec((1, CONV_W - 1, r), lambda i, j: (i, 0, 0))],
        out_shape=[jax.ShapeDtypeStruct((b, t, d), F32),
                   jax.ShapeDtypeStruct((b, 1, r), F32),
                   jax.ShapeDtypeStruct((b, CONV_W - 1, r), F32)],
        scratch_shapes=[pltpu.VMEM((SUBLANE + tc, r), F32), pltpu.VMEM((tc, r), F32),
                        pltpu.VMEM((tc, r), F32), pltpu.VMEM((tc, r), F32), pltpu.VMEM((SUBLANE, r), F32)],
        compiler_params=_cparams("arbitrary", "arbitrary"),
        name="rec_prompt",
    )(x, g.reshape(1, d), *wargs)
    return h, hl.reshape(b, r), cn


def _rec_sample(x, g, wins, wargs, conv0, h0):
    m, d = x.shape
    r = wargs[-1].shape[0]
    kern = functools.partial(_rec_sample_kernel, wins=wins, r=r)
    h, hl, u = pl.pallas_call(
        kern,
        out_shape=[jax.ShapeDtypeStruct((m, d), F32), jax.ShapeDtypeStruct((m, r), F32),
                   jax.ShapeDtypeStruct((m, r), F32)],
        compiler_params=pltpu.CompilerParams(vmem_limit_bytes=VMEM_LIMIT),
        name="rec_sample",
    )(x, g.reshape(1, d), *wargs, conv0[:, 0], conv0[:, 1], conv0[:, 2], h0)
    conv_new = jnp.stack([conv0[:, 1], conv0[:, 2], u], axis=1)
    return h, hl, conv_new


def _ple(h, p_ref, g_ref, wgate_ref, wproj_ref):
    gate = jax.nn.sigmoid(_dot(_rms(h, g_ref[...]).astype(BF16), wgate_ref[...]))
    return h + gate * _dot(p_ref[...].astype(BF16), wproj_ref[...])


def _ffn_dense_kernel(h_ref, g_ref, wg_ref, wu_ref, wd_ref, p_ref, gp_ref, wpg_ref, wpp_ref,
                      o_ref, hn_s, acc_s):
    j = pl.program_id(1)

    @pl.when(j == 0)
    def _():
        hn_s[...] = _rms(h_ref[...], g_ref[...]).astype(BF16)
        acc_s[...] = jnp.zeros(acc_s.shape, F32)

    hn = hn_s[...]
    gate = _dot(hn, wg_ref[...])
    up = _dot(hn, wu_ref[...])
    act = (jax.nn.silu(gate) * up).astype(BF16)
    acc_s[...] += _dot(act, wd_ref[...])

    @pl.when(j == pl.num_programs(1) - 1)
    def _():
        o_ref[...] = _ple(h_ref[...] + acc_s[...], p_ref, gp_ref, wpg_ref, wpp_ref)


def _ffn_dense(h, g, w_gu, w_down, p, g_ple, w_pgate, w_pproj):
    m, d = h.shape
    f = w_down.shape[0]
    pd = p.shape[1]
    tm = _tile(m, 512)
    tf = _tile(f, 512, LANE)
    nf = f // tf
    return pl.pallas_call(
        _ffn_dense_kernel,
        grid=(m // tm, nf),
        in_specs=[pl.BlockSpec((tm, d), lambda i, j: (i, 0)), _full((1, d)),
                  pl.BlockSpec((d, tf), lambda i, j: (0, j)),
                  pl.BlockSpec((d, tf), lambda i, j: (0, nf + j)),
                  pl.BlockSpec((tf, d), lambda i, j: (j, 0)),
                  pl.BlockSpec((tm, pd), lambda i, j: (i, 0)), _full((1, d)),
                  _full((d, d)), _full((pd, d))],
        out_specs=pl.BlockSpec((tm, d), lambda i, j: (i, 0)),
        out_shape=jax.ShapeDtypeStruct((m, d), F32),
        scratch_shapes=[pltpu.VMEM((tm, d), BF16), pltpu.VMEM((tm, d), F32)],
        compiler_params=_cparams("arbitrary", "arbitrary"),
        name="ffn_dense",
    )(h, g.reshape(1, d), w_gu, w_gu, w_down, p, g_ple.reshape(1, d), w_pgate, w_pproj)


def _kvq_kernel(h_ref, gkv_ref, gq_ref, wkv_ref, bf_ref, wq_ref, k_ref, v_ref, q_ref, lf_ref,
                *, qk, nh, head_major):
    x = h_ref[...]
    xh = x * lax.rsqrt(jnp.mean(x * x, axis=-1, keepdims=True) + EPS)
    kvf = _dot((xh * gkv_ref[...]).astype(BF16), wkv_ref[...])
    k_ref[...] = kvf[:, :qk]
    v_ref[...] = kvf[:, qk:2 * qk]
    lf = jax.nn.log_sigmoid(kvf[:, 2 * qk:] + bf_ref[...])
    if head_major:
        lf_ref[0] = lf.T[:nh, :]
    else:
        lf_ref[...] = lf[:, :nh]
    q_ref[...] = _dot((xh * gq_ref[...]).astype(BF16), wq_ref[...]).astype(BF16)


def _kvq(h, g_kv, g_q, w_kv, b_f, w_q, *, seq_len=None):
    m, d = h.shape
    qk = w_q.shape[1]
    nh = b_f.shape[0]
    head_major = seq_len is not None
    tm = _tile(seq_len if head_major else m, 512, LANE if head_major else SUBLANE)
    wkv = jnp.pad(w_kv, ((0, 0), (0, LANE - nh))).astype(BF16)
    bfp = jnp.pad(b_f, (0, LANE - nh)).reshape(1, LANE)
    out_specs = [pl.BlockSpec((tm, qk), lambda i: (i, 0)), pl.BlockSpec((tm, qk), lambda i: (i, 0)),
                 pl.BlockSpec((tm, qk), lambda i: (i, 0))]
    out_shape = [jax.ShapeDtypeStruct((m, qk), F32), jax.ShapeDtypeStruct((m, qk), F32),
                 jax.ShapeDtypeStruct((m, qk), BF16)]
    if head_major:
        nt = seq_len // tm
        out_specs.append(pl.BlockSpec((1, nh, tm), lambda i: (i // nt, 0, i % nt)))
        out_shape.append(jax.ShapeDtypeStruct((m // seq_len, nh, seq_len), F32))
    else:
        out_specs.append(pl.BlockSpec((tm, nh), lambda i: (i, 0)))
        out_shape.append(jax.ShapeDtypeStruct((m, nh), F32))
    kern = functools.partial(_kvq_kernel, qk=qk, nh=nh, head_major=head_major)
    return pl.pallas_call(
        kern,
        grid=(m // tm,),
        in_specs=[pl.BlockSpec((tm, d), lambda i: (i, 0)), _full((1, d)), _full((1, d)),
                  _full(wkv.shape), _full((1, LANE)), _full((d, qk))],
        out_specs=out_specs,
        out_shape=out_shape,
        compiler_params=_cparams("arbitrary"),
        name="kvq",
    )(h, g_kv.reshape(1, d), g_q.reshape(1, d), wkv, bfp, w_q.astype(BF16))


def _cumsum_lanes(x):
    n = x.shape[1]
    lane = lax.broadcasted_iota(jnp.int32, x.shape, 1)
    k = 1
    while k < n:
        x = x + jnp.where(lane >= k, pltpu.roll(x, k, 1), 0.0)
        k *= 2
    return x


def _fold_lanes(x, op):
    r = x[:, :LANE]
    for c in range(1, x.shape[1] // LANE):
        r = op(r, x[:, c * LANE:(c + 1) * LANE])
    return r


def _attn_prompt_kernel(q_ref, k_ref, v_ref, lft_ref, o_ref, kb_s, vb_s, c2_s, s_s, p_s, *, tq, scale):
    h = pl.program_id(1)
    iq = pl.program_id(2)
    nblk = c2_s.shape[0]

    @pl.when(iq == 0)
    def _():
        kb_s[...] = k_ref[0].astype(BF16)
        vb_s[...] = v_ref[0].astype(BF16)

    @pl.when((iq == 0) & (h == 0))
    def _():
        c = _cumsum_lanes(lft_ref[0]) * LOG2E
        for j in range(nblk):
            c2_s[j] = c[:, j * tq:(j + 1) * tq]

    q = q_ref[0]
    hd = q.shape[1]
    c1 = scale * LOG2E

    def attend(n_kb):
        mrun = jnp.full((tq, LANE), NEG_BIG, F32)
        for kb in range(n_kb):
            s = lax.dot_general(q, kb_s[kb * tq:(kb + 1) * tq, :], (((1,), (1,)), ((), ())),
                                preferred_element_type=F32) * c1
            s = s - c2_s[kb, pl.ds(h, 1), :]
            if kb == n_kb - 1:
                row = lax.broadcasted_iota(jnp.int32, (tq, tq), 0)
                col = lax.broadcasted_iota(jnp.int32, (tq, tq), 1)
                s = jnp.where(col <= row, s, NEG_BIG)
            s_s[kb] = s
            mrun = jnp.maximum(mrun, _fold_lanes(s, jnp.maximum))
        m = jnp.max(mrun, axis=1, keepdims=True)
        lrun = jnp.zeros((tq, LANE), F32)
        for kb in range(n_kb):
            p = jnp.exp2(s_s[kb] - m)
            lrun = lrun + _fold_lanes(p, jnp.add)
            p_s[:, kb * tq:(kb + 1) * tq] = p.astype(BF16)
        acc = _dot(p_s[:, :n_kb * tq], vb_s[:n_kb * tq, :])
        o_ref[0] = (acc / jnp.sum(lrun, axis=1, keepdims=True)).astype(BF16)

    for n in range(nblk):
        pl.when(iq == n)(functools.partial(attend, n + 1))


def _attn_prompt(q, k, v, lft, *, nh, scale):
    b, t, qk = q.shape
    hd = qk // nh
    tq = _tile(t, 256, LANE)
    nq = t // tq
    kern = functools.partial(_attn_prompt_kernel, tq=tq, scale=scale)
    return pl.pallas_call(
        kern,
        grid=(b, nh, nq),
        in_specs=[pl.BlockSpec((1, tq, hd), lambda i, h, j: (i, j, h)),
                  pl.BlockSpec((1, t, hd), lambda i, h, j: (i, 0, h)),
                  pl.BlockSpec((1, t, hd), lambda i, h, j: (i, 0, h)),
                  pl.BlockSpec((1, nh, t), lambda i, h, j: (i, 0, 0))],
        out_specs=pl.BlockSpec((1, tq, hd), lambda i, h, j: (i, j, h)),
        out_shape=jax.ShapeDtypeStruct((b, t, qk), BF16),
        scratch_shapes=[pltpu.VMEM((t, hd), BF16), pltpu.VMEM((t, hd), BF16),
                        pltpu.VMEM((nq, nh, tq), F32), pltpu.VMEM((nq, tq, tq), F32),
                        pltpu.VMEM((tq, t), BF16)],
        compiler_params=_cparams("arbitrary", "arbitrary", "arbitrary"),
        name="attn_prompt",
    )(q, k, v, lft)


def _suffix_sums_lanes(x):
    n = x.shape[1]
    lane = lax.broadcasted_iota(jnp.int32, x.shape, 1)
    k = 1
    while k < n:
        x = x + jnp.where(lane < n - k, pltpu.roll(x, n - k, 1), 0.0)
        k *= 2
    return x


def _attn_sample_kernel(pt_ref, q_ref, kn_ref, vn_ref, lfn_ref, spread_ref, *rest, pp, nh, scale):
    k_refs = rest[0:pp]
    v_refs = rest[pp:2 * pp]
    lf_refs = rest[2 * pp:3 * pp]
    o_ref = rest[3 * pp]
    m_s, l_s, acc_s, carry_s = rest[3 * pp + 1:]
    del pt_ref
    p = pl.program_id(1)
    q = q_ref[0].astype(BF16)

    @pl.when(p == 0)
    def _():
        kn = kn_ref[0].astype(BF16).astype(F32)
        s_new = jnp.sum(q.astype(F32) * kn, axis=1, keepdims=True) * scale
        m_s[...] = jnp.broadcast_to(s_new, m_s.shape)
        l_s[...] = jnp.ones(l_s.shape, F32)
        acc_s[...] = vn_ref[0]
        carry_s[...] = lfn_ref[0]

    carry = carry_s[...]
    pieces = []
    for j in range(pp):
        lf = lf_refs[j][0]
        suffix = _suffix_sums_lanes(lf)
        pieces.extend(_split3_f32(suffix - lf + carry))
        carry = carry + suffix[:, 0:1]
    carry_s[...] = carry
    spread = _dot(jnp.concatenate(pieces, axis=0).astype(BF16), spread_ref[...])
    bias = jnp.concatenate(
        [spread[3 * j * nh:(3 * j + 1) * nh] + spread[(3 * j + 1) * nh:(3 * j + 2) * nh]
         + spread[(3 * j + 2) * nh:(3 * j + 3) * nh] for j in range(pp)], axis=1)

    k_all = jnp.concatenate([r[0] for r in k_refs], axis=0).astype(BF16)
    s = lax.dot_general(q, k_all, (((1,), (1,)), ((), ())), preferred_element_type=F32)
    row = lax.broadcasted_iota(jnp.int32, s.shape, 0)
    lane = lax.broadcasted_iota(jnp.int32, s.shape, 1)
    own_head = jnp.bitwise_and(lane, nh - 1) == row
    s = jnp.where(own_head, s * scale + bias, NEG_BIG)

    m_old = m_s[...]
    m_new = jnp.maximum(m_old, jnp.max(s, axis=1, keepdims=True))
    alpha = jnp.exp(m_old - m_new)
    pr = jnp.exp(s - m_new[:, 0:1])
    m_s[...] = m_new
    l_s[...] = alpha * l_s[...] + jnp.sum(pr, axis=1, keepdims=True)
    v_all = jnp.concatenate([r[0] for r in v_refs], axis=0).astype(BF16)
    acc_s[...] = alpha * acc_s[...] + _dot(pr.astype(BF16), v_all)

    @pl.when(p == pl.num_programs(1) - 1)
    def _():
        o_ref[0] = acc_s[...] / l_s[...]


def _attn_sample(q, k_new, v_new, lf_new, cache_k, cache_v, cache_logf, page_table, *, scale):
    b, qk = q.shape
    n_phys, ps, nh, hd = cache_k.shape
    assert ps == LANE and hd == LANE and nh == SUBLANE
    n_pages = page_table.shape[1]
    pp = 8
    while n_pages % pp:
        pp //= 2
    ck = cache_k.reshape(n_phys, ps * nh, hd)
    cv = cache_v.reshape(n_phys, ps * nh, hd)
    clf = jnp.swapaxes(cache_logf, 1, 2)
    pt = page_table.reshape(-1)
    lfn = jnp.broadcast_to(lf_new[:, :, None], (b, nh, LANE))
    spread = (jnp.arange(ps)[:, None] == (jnp.arange(ps * nh) // nh)[None, :]).astype(BF16)

    def page_map(j):
        return lambda i, p, pt_ref: (pt_ref[i * n_pages + (n_pages - 1 - (p * pp + j))], 0, 0)

    head_spec = pl.BlockSpec((1, nh, hd), lambda i, p, pt_ref: (i, 0, 0))
    in_specs = [head_spec] * 4 + [pl.BlockSpec((ps, ps * nh), lambda i, p, pt_ref: (0, 0))]
    in_specs += [pl.BlockSpec((1, ps * nh, hd), page_map(j)) for j in range(pp)]
    in_specs += [pl.BlockSpec((1, ps * nh, hd), page_map(j)) for j in range(pp)]
    in_specs += [pl.BlockSpec((1, nh, ps), page_map(j)) for j in range(pp)]
    kern = functools.partial(_attn_sample_kernel, pp=pp, nh=nh, scale=scale)
    out = pl.pallas_call(
        kern,
        grid_spec=pltpu.PrefetchScalarGridSpec(
            num_scalar_prefetch=1,
            grid=(b, n_pages // pp),
            in_specs=in_specs,
            out_specs=head_spec,
            scratch_shapes=[pltpu.VMEM((nh, LANE), F32), pltpu.VMEM((nh, LANE), F32),
                            pltpu.VMEM((nh, hd), F32), pltpu.VMEM((nh, LANE), F32)]),
        out_shape=jax.ShapeDtypeStruct((b, nh, hd), F32),
        compiler_params=_cparams("arbitrary", "arbitrary"),
        name="attn_sample",
    )(pt, q.reshape(b, nh, hd), k_new.reshape(b, nh, hd), v_new.reshape(b, nh, hd), lfn, spread,
      *([ck] * pp), *([cv] * pp), *([clf] * pp))
    return out.reshape(b, qk)


def _oproj_router_kernel(a_ref, h_ref, wo_ref, g_ref, wr_ref, br_ref, h4_ref, hn_ref, route_ref, cnt_ref,
                         carry_s, *, n_exp):
    i = pl.program_id(0)

    @pl.when(i == 0)
    def _():
        carry_s[...] = jnp.zeros(carry_s.shape, F32)

    h4 = h_ref[...] + _dot(a_ref[...].astype(BF16), wo_ref[...])
    h4_ref[...] = h4
    hn = _rms(h4, g_ref[...])
    hn_ref[...] = hn
    tm = hn.shape[0]

    x_hi, x_mid, _ = _split3(hn)
    w_hi, w_mid, _ = _split3(wr_ref[...])
    logits = _dot(x_hi, w_hi) + _dot(x_mid, w_hi) + _dot(x_hi, w_mid) + br_ref[...]

    lane = lax.broadcasted_iota(jnp.int32, (tm, LANE), 1)
    valid = lane < n_exp
    logits = jnp.where(valid, logits, NEG_BIG)
    ex = jnp.exp(logits - jnp.max(logits, axis=-1, keepdims=True))
    probs = jnp.where(valid, ex / jnp.sum(ex, axis=-1, keepdims=True), -1.0)
    p1 = jnp.max(probs, axis=-1, keepdims=True)
    i1 = jnp.min(jnp.where(probs == p1, lane, LANE), axis=-1, keepdims=True)
    probs2 = jnp.where(lane == i1, -1.0, probs)
    p2 = jnp.max(probs2, axis=-1, keepdims=True)
    i2 = jnp.min(jnp.where(probs2 == p2, lane, LANE), axis=-1, keepdims=True)
    den = p1 + p2
    w1 = p1 / den
    w2 = p2 / den

    onehot = ((lane == i1) | (lane == i2)).astype(F32)
    r_i = lax.broadcasted_iota(jnp.int32, (tm, tm), 0)
    c_i = lax.broadcasted_iota(jnp.int32, (tm, tm), 1)
    strict_lower = (c_i < r_i).astype(BF16)
    ranks = _dot(strict_lower, onehot.astype(BF16)) + carry_s[0:1, :]
    r1 = jnp.sum(jnp.where(lane == i1, ranks, 0.0), axis=-1, keepdims=True)
    r2 = jnp.sum(jnp.where(lane == i2, ranks, 0.0), axis=-1, keepdims=True)
    carry_s[0:1, :] = carry_s[0:1, :] + jnp.sum(onehot, axis=0, keepdims=True)
    cnt_ref[...] = jnp.broadcast_to(carry_s[0:1, :], cnt_ref.shape)

    route = jnp.where(lane == 0, i1.astype(F32), 0.0)
    route = jnp.where(lane == 1, i2.astype(F32), route)
    route = jnp.where(lane == 2, w1, route)
    route = jnp.where(lane == 3, w2, route)
    route = jnp.where(lane == 4, r1, route)
    route = jnp.where(lane == 5, r2, route)
    route_ref[...] = route


def _oproj_router(attn, h, w_o, g, w_router, b_router):
    m, d = h.shape
    qk = attn.shape[1]
    n_exp = w_router.shape[1]
    tm = _tile(m, 512)
    wr = jnp.pad(w_router, ((0, 0), (0, LANE - n_exp)))
    br = jnp.pad(b_router, (0, LANE - n_exp)).reshape(1, LANE)
    kern = functools.partial(_oproj_router_kernel, n_exp=n_exp)
    row = lambda w: pl.BlockSpec((tm, w), lambda i: (i, 0))
    return pl.pallas_call(
        kern,
        grid=(m // tm,),
        in_specs=[row(qk), row(d), _full((qk, d)), _full((1, d)), _full((d, LANE)), _full((1, LANE))],
        out_specs=[row(d), row(d), row(LANE), _full((SUBLANE, LANE))],
        out_shape=[jax.ShapeDtypeStruct((m, d), F32), jax.ShapeDtypeStruct((m, d), F32),
                   jax.ShapeDtypeStruct((m, LANE), F32), jax.ShapeDtypeStruct((SUBLANE, LANE), F32)],
        scratch_shapes=[pltpu.VMEM((SUBLANE, LANE), F32)],
        compiler_params=_cparams("arbitrary"),
        name="oproj_router",
    )(attn, h, w_o.astype(BF16), g.reshape(1, d), wr, br)


def _row_copy_wait(src, dst, sem, rows):
    pltpu.make_async_copy(src.at[pl.ds(0, rows)], dst.at[pl.ds(0, rows)], sem).wait()


def _dispatch_kernel(pos_ref, hn_ref, xs_in_ref, xs_ref, sem, *, tm):
    del xs_in_ref
    i = pl.program_id(0)

    def body(r, c):
        for k in range(TOP_K):
            dst = pos_ref[(i * tm + r) * TOP_K + k]
            pltpu.make_async_copy(hn_ref.at[pl.ds(r, 1)], xs_ref.at[pl.ds(dst, 1)], sem).start()
        return c

    lax.fori_loop(0, tm, body, 0)
    for _ in range(TOP_K):
        _row_copy_wait(hn_ref, xs_ref, sem, tm)


def _dispatch(hn, pos, n_sorted):
    m, d = hn.shape
    tm = _tile(m, 512)
    kern = functools.partial(_dispatch_kernel, tm=tm)
    return pl.pallas_call(
        kern,
        grid_spec=pltpu.PrefetchScalarGridSpec(
            num_scalar_prefetch=1,
            grid=(m // tm,),
            in_specs=[pl.BlockSpec((tm, d), lambda i, pos_ref: (i, 0)),
                      pl.BlockSpec(memory_space=pl.ANY)],
            out_specs=pl.BlockSpec(memory_space=pl.ANY),
            scratch_shapes=[pltpu.SemaphoreType.DMA(())]),
        out_shape=jax.ShapeDtypeStruct((n_sorted, d), F32),
        input_output_aliases={2: 0},
        compiler_params=_cparams("arbitrary"),
        name="moe_dispatch",
    )(pos, hn, jnp.zeros((n_sorted, d), F32))


def _experts_kernel(te_ref, tv_ref, x_ref, wg_ref, wu_ref, wd_ref, y_ref, xb_s, acc_s):
    del te_ref
    i = pl.program_id(0)
    j = pl.program_id(1)
    valid = tv_ref[i] > 0

    @pl.when(valid & (j == 0))
    def _():
        xb_s[...] = x_ref[...].astype(BF16)
        acc_s[...] = jnp.zeros(acc_s.shape, F32)

    @pl.when(valid)
    def _():
        x = xb_s[...]
        gate = _dot(x, wg_ref[0].astype(BF16))
        up = _dot(x, wu_ref[0].astype(BF16))
        act = (jax.nn.silu(gate) * up).astype(BF16)
        acc_s[...] += _dot(act, wd_ref[0].astype(BF16))

    last = j == pl.num_programs(1) - 1

    @pl.when(valid & last)
    def _():
        y_ref[...] = acc_s[...]

    @pl.when(jnp.logical_not(valid) & last)
    def _():
        y_ref[...] = jnp.zeros(y_ref.shape, F32)


def _experts(xs, tile_expert, tile_valid, w_gu, w_down, tm):
    ns, d = xs.shape
    f = w_down.shape[1]
    tf = _tile(f, 512, LANE)
    nf = f // tf
    nt = ns // tm

    def jj(i, j, tv):
        return jnp.where(tv[i] > 0, j, nf - 1)

    return pl.pallas_call(
        _experts_kernel,
        grid_spec=pltpu.PrefetchScalarGridSpec(
            num_scalar_prefetch=2,
            grid=(nt, nf),
            in_specs=[pl.BlockSpec((tm, d), lambda i, j, te, tv: (i, 0)),
                      pl.BlockSpec((1, d, tf), lambda i, j, te, tv: (te[i], 0, jj(i, j, tv))),
                      pl.BlockSpec((1, d, tf), lambda i, j, te, tv: (te[i], 0, nf + jj(i, j, tv))),
                      pl.BlockSpec((1, tf, d), lambda i, j, te, tv: (te[i], jj(i, j, tv), 0))],
            out_specs=pl.BlockSpec((tm, d), lambda i, j, te, tv: (i, 0)),
            scratch_shapes=[pltpu.VMEM((tm, d), BF16), pltpu.VMEM((tm, d), F32)]),
        out_shape=jax.ShapeDtypeStruct((ns, d), F32),
        compiler_params=_cparams("arbitrary", "arbitrary"),
        name="moe_experts",
    )(tile_expert, tile_valid, xs, w_gu, w_gu, w_down)


def _combine_kernel(pos_ref, ys_ref, h_ref, route_ref, p_ref, gp_ref, wpg_ref, wpp_ref, gf_ref, o_ref,
                    buf_s, sem, *, tm):
    i = pl.program_id(0)
    n = pl.num_programs(0)

    def issue(step, slot):
        def body(r, c):
            for k in range(TOP_K):
                src = pos_ref[(step * tm + r) * TOP_K + k]
                pltpu.make_async_copy(ys_ref.at[pl.ds(src, 1)], buf_s.at[slot, k, pl.ds(r, 1)],
                                      sem.at[slot]).start()
            return c
        lax.fori_loop(0, tm, body, 0)

    @pl.when(i == 0)
    def _():
        issue(0, 0)

    slot = lax.rem(i, 2)

    @pl.when(i + 1 < n)
    def _():
        issue(i + 1, 1 - slot)

    for k in range(TOP_K):
        _row_copy_wait(ys_ref, buf_s.at[slot, k], sem.at[slot], tm)

    route = route_ref[...]
    h = h_ref[...] + route[:, 2:3] * buf_s[slot, 0] + route[:, 3:4] * buf_s[slot, 1]
    h = _ple(h, p_ref, gp_ref, wpg_ref, wpp_ref)
    o_ref[...] = _rms(h, gf_ref[...])


def _combine(ys, pos, h, route, p, g_ple, w_pgate, w_pproj, g_final):
    m, d = h.shape
    pd = p.shape[1]
    tm = _tile(m, 256)
    kern = functools.partial(_combine_kernel, tm=tm)
    row = lambda w: pl.BlockSpec((tm, w), lambda i, pos_ref: (i, 0))
    const = lambda s: pl.BlockSpec(s, lambda i, pos_ref: (0, 0))
    return pl.pallas_call(
        kern,
        grid_spec=pltpu.PrefetchScalarGridSpec(
            num_scalar_prefetch=1,
            grid=(m // tm,),
            in_specs=[pl.BlockSpec(memory_space=pl.ANY), row(d), row(LANE), row(pd), const((1, d)),
                      const((d, d)), const((pd, d)), const((1, d))],
            out_specs=row(d),
            scratch_shapes=[pltpu.VMEM((2, TOP_K, tm, d), F32), pltpu.SemaphoreType.DMA((2,))]),
        out_shape=jax.ShapeDtypeStruct((m, d), F32),
        compiler_params=_cparams("arbitrary"),
        name="moe_combine",
    )(pos, ys, h, route, p, g_ple.reshape(1, d), w_pgate, w_pproj, g_final.reshape(1, d))


def _route_plan(route, counts, n_exp, tm, n_tiles):
    e = route[:, 0:TOP_K].astype(jnp.int32)
    rank = route[:, 4:4 + TOP_K].astype(jnp.int32)
    cnt = counts[0, :n_exp].astype(jnp.int32)
    ntile = (cnt + tm - 1) // tm
    tile_end = jnp.cumsum(ntile)
    row_start = (tile_end - ntile) * tm
    sel = e[:, :, None] == jnp.arange(n_exp)[None, None, :]
    pos = jnp.sum(jnp.where(sel, row_start[None, None, :], 0), axis=-1) + rank
    tiles = jnp.arange(n_tiles)
    total = tile_end[-1]
    te = jnp.minimum(jnp.sum(tiles[:, None] >= tile_end[None, :], axis=1), n_exp - 1)
    valid = tiles < total
    te_last = jnp.sum(jnp.where(tiles == total - 1, te, 0))
    te = jnp.where(valid, te, te_last).astype(jnp.int32)
    return pos.reshape(-1).astype(jnp.int32), te, valid.astype(jnp.int32)


def _moe_and_final(attn, h, w_o, g_ffn, w_router, b_router, w_gu, w_down, p, g_ple, w_pgate, w_pproj,
                   g_final, *, tm_expert):
    m, d = h.shape
    n_exp = w_router.shape[1]
    h4, hn, route, counts = _oproj_router(attn, h, w_o, g_ffn, w_router, b_router)
    n_tiles = -(-(m * TOP_K) // tm_expert) + n_exp
    pos, te, tv = _route_plan(route, counts, n_exp, tm_expert, n_tiles)
    xs = _dispatch(hn, pos, n_tiles * tm_expert)
    ys = _experts(xs, te, tv, w_gu, w_down, tm_expert)
    return _combine(ys, pos, h4, route, p, g_ple, w_pgate, w_pproj, g_final)


def kernel(x_prompt, x_sample, state_h, state_conv, cache_k, cache_v, cache_logf, page_table, p_prompt, p_sample, g_mix_a, w_in_a, conv_w_a, conv_b_a, w_rg_a, b_rg_a, w_ig_a, b_ig_a, lru_lambda_a, w_out_a, g_kv, w_kv, b_f, g_mix_b, w_q_b, w_o_b, g_ffn, w_gu_dense, w_down_dense, w_router, b_router, w_gu_moe, w_down_moe, g_ple, w_ple_gate, w_ple_proj, g_final):
    assert w_in_a.shape[0] == 1 and w_q_b.shape[0] == 1 and g_ffn.shape[0] == 2
    assert x_sample.shape[1] == 1
    bsz, t_len, d = x_prompt.shape
    dec_b = x_sample.shape[0]
    nh = b_f.shape[0]
    qk = w_q_b.shape[2]
    hd = qk // nh
    r = w_out_a.shape[1]
    scale = hd ** -0.5
    mp = bsz * t_len

    wins, rec_w = _rec_weights(w_in_a[0], conv_w_a[0], conv_b_a[0], w_rg_a[0], b_rg_a[0], w_ig_a[0],
                               b_ig_a[0], lru_lambda_a[0], w_out_a[0])
    w_gu_d = w_gu_dense[0].astype(BF16)
    w_down_d = w_down_dense[0].astype(BF16)
    w_pg = [w_ple_gate[l].astype(BF16) for l in range(2)]
    w_pp = [w_ple_proj[l].astype(BF16) for l in range(2)]

    def rest_of_step(h1, p, seq_len):
        h3 = _ffn_dense(h1, g_ffn[0], w_gu_d, w_down_d, p[0], g_ple[0], w_pg[0], w_pp[0])
        return h3, _kvq(h3, g_kv, g_mix_b[0], w_kv, b_f, w_q_b[0], seq_len=seq_len)

    def moe(attn, h3, p, tm_expert):
        return _moe_and_final(attn, h3, w_o_b[0], g_ffn[1], w_router[0], b_router[0], w_gu_moe[0],
                              w_down_moe[0], p[1], g_ple[1], w_pg[1], w_pp[1], g_final, tm_expert=tm_expert)

    h1, hl_p, cn_p = _rec_prompt(x_prompt, g_mix_a[0], wins, rec_w)
    pp = p_prompt.reshape(2, mp, -1)
    h3, (k_p, v_p, q_p, lft_p) = rest_of_step(h1.reshape(mp, d), pp, t_len)
    attn_p = _attn_prompt(q_p.reshape(bsz, t_len, qk), k_p.reshape(bsz, t_len, qk), v_p.reshape(bsz, t_len, qk),
                          lft_p, nh=nh, scale=scale)
    lf_p = jnp.swapaxes(lft_p, 1, 2)
    y_p = moe(attn_p.reshape(mp, qk), h3, pp, min(1024, mp))

    h1s, hl_s, cn_s = _rec_sample(x_sample.reshape(dec_b, d), g_mix_a[0], wins, rec_w, state_conv[0], state_h[0])
    ps = p_sample.reshape(2, dec_b, -1)
    h3s, (k_s, v_s, q_s, lf_s) = rest_of_step(h1s, ps, None)
    attn_s = _attn_sample(q_s.astype(F32), k_s, v_s, lf_s, cache_k, cache_v, cache_logf, page_table, scale=scale)
    y_s = moe(attn_s, h3s, ps, dec_b)

    return (y_p.reshape(bsz, t_len, d), y_s.reshape(dec_b, 1, d),
            hl_p[None], cn_p[None],
            k_p.reshape(bsz, t_len, nh, hd), v_p.reshape(bsz, t_len, nh, hd), lf_p,
            hl_s[None], cn_s[None],
            k_s.reshape(dec_b, 1, nh, hd), v_s.reshape(dec_b, 1, nh, hd), lf_s.reshape(dec_b, 1, nh))
```

```python
import functools

import jax
import jax.numpy as jnp
from jax import lax
from jax.experimental import pallas as pl
from jax.experimental.pallas import tpu as pltpu

F32 = jnp.float32
BF16 = jnp.bfloat16

EPS = 1e-6
LRU_C = 8.0
N_LRU_BLOCKS = 8
CONV_W = 4
N_EXPERTS = 8
TOP_K = 2

LANE = 128
SUBLANE = 8
MXU_WIDTH = 256
VMEM_BYTES = 64 * 1024 * 1024
VMEM_LIMIT = VMEM_BYTES * 7 // 8
NEG_BIG = -1e30
LOG2E = 1.4426950408889634


def _cparams(*sem):
    return pltpu.CompilerParams(dimension_semantics=sem, vmem_limit_bytes=VMEM_LIMIT)


def _tile(n, pref, align=SUBLANE):
    if n <= pref:
        return n
    t = pref // align * align
    while n % t:
        t -= align
    return t


def _full(shape):
    nd = len(shape)
    return pl.BlockSpec(shape, lambda *_: (0,) * nd)


def _rms(x, g):
    ms = jnp.mean(x * x, axis=-1, keepdims=True)
    return x * lax.rsqrt(ms + EPS) * g


def _dot(a, b):
    return jnp.dot(a, b, preferred_element_type=F32)


def _split3_f32(x):
    hi = x.astype(BF16).astype(F32)
    r1 = x - hi
    mid = r1.astype(BF16).astype(F32)
    lo = (r1 - mid).astype(BF16).astype(F32)
    return hi, mid, lo


def _split3(x):
    return tuple(piece.astype(BF16) for piece in _split3_f32(x))


def _band_windows(r, nblk, tw):
    blk = r // nblk
    wins = []
    c0 = 0
    while c0 < r:
        c1 = min(c0 + tw, r)
        b0, b1 = c0 // blk, (c1 - 1) // blk
        k0 = (b0 * blk) // LANE * LANE
        k1 = min(r, -(-((b1 + 1) * blk) // LANE) * LANE)
        wins.append((c0, c1, k0, k1))
        c0 = c1
    return tuple(wins)


def _band_pack(w, wins, tw):
    n, c, e = w.shape
    r = n * c
    dense = jnp.einsum('nce,nm->ncme', w, jnp.eye(n, dtype=w.dtype)).reshape(r, r)
    parts = []
    for c0, c1, k0, k1 in wins:
        parts.append(jnp.pad(dense[k0:k1, c0:c1], ((0, 0), (0, tw - (c1 - c0)))))
    return jnp.concatenate(parts, axis=0).astype(BF16)


def _band_matmul(x_bf, w_ref, wins):
    outs = []
    off = 0
    for c0, c1, k0, k1 in wins:
        o = _dot(x_bf[:, k0:k1], w_ref[off:off + (k1 - k0), :])
        outs.append(o[:, :c1 - c0])
        off += k1 - k0
    return jnp.concatenate(outs, axis=1) if len(outs) > 1 else outs[0]


def _lru_gates(uc, rg, ig, lam):
    r = jax.nn.sigmoid(rg)
    i = jax.nn.sigmoid(ig)
    log_a = (-LRU_C) * r * jax.nn.softplus(-lam)
    a = jnp.exp(log_a)
    one_minus_a2 = jnp.tanh(-log_a) * (a * a + 1.0)
    return a, jnp.sqrt(one_minus_a2) * (i * uc)


def _rec_prompt_kernel(x_ref, g_ref, win_ref, cw_ref, cb_ref, wrg_ref, brg_ref, wig_ref, big_ref,
                       lam_ref, wout_ref, h_ref, hl_ref, cn_ref, uh_s, a_s, xi_s, hs_s, hc_s,
                       *, wins, tc, r):
    t = pl.program_id(1)
    halo = SUBLANE

    @pl.when(t == 0)
    def _():
        uh_s[0:halo, :] = jnp.zeros((halo, r), F32)
        hc_s[...] = jnp.zeros(hc_s.shape, F32)

    x = x_ref[0]
    xn = _rms(x, g_ref[...]).astype(BF16)
    gu = _dot(xn, win_ref[...])
    gate = gu[:, :r]
    uh_s[halo:halo + tc, :] = gu[:, r:]
    base = halo - (CONV_W - 1)
    uc = cb_ref[...] + uh_s[base:base + tc, :] * cw_ref[0:1, :]
    for j in range(1, CONV_W):
        uc = uc + uh_s[base + j:base + j + tc, :] * cw_ref[j:j + 1, :]

    @pl.when(t == pl.num_programs(1) - 1)
    def _():
        cn_ref[0] = uh_s[halo + tc - (CONV_W - 1):halo + tc, :]

    uh_s[0:halo, :] = uh_s[tc:tc + halo, :]

    uc_bf = uc.astype(BF16)
    rg = _band_matmul(uc_bf, wrg_ref, wins) + brg_ref[...]
    ig = _band_matmul(uc_bf, wig_ref, wins) + big_ref[...]
    a, xin = _lru_gates(uc, rg, ig, lam_ref[...])
    a_s[...] = a
    xi_s[...] = xin

    row = lax.broadcasted_iota(jnp.int32, (SUBLANE, r), 0)

    def body(gi, h_in):
        off = pl.multiple_of(gi * SUBLANE, SUBLANE)
        a8 = a_s[pl.ds(off, SUBLANE), :]
        x8 = xi_s[pl.ds(off, SUBLANE), :]
        for k in (1, 2, 4):
            m = row >= k
            x8 = x8 + a8 * jnp.where(m, pltpu.roll(x8, k, 0), 0.0)
            a8 = a8 * jnp.where(m, pltpu.roll(a8, k, 0), 1.0)
        h8 = a8 * h_in + x8
        hs_s[pl.ds(off, SUBLANE), :] = h8
        return h8[SUBLANE - 1:SUBLANE, :]

    h_last = lax.fori_loop(0, tc // SUBLANE, body, hc_s[0:1, :])
    hc_s[0:1, :] = h_last
    hl_ref[0] = h_last

    y = (hs_s[...] * jax.nn.gelu(gate)).astype(BF16)
    h_ref[0] = x + _dot(y, wout_ref[...])


def _rec_sample_kernel(x_ref, g_ref, win_ref, cw_ref, cb_ref, wrg_ref, brg_ref, wig_ref, big_ref,
                       lam_ref, wout_ref, c0_ref, c1_ref, c2_ref, h0_ref, h_ref, hl_ref, u_ref,
                       *, wins, r):
    x = x_ref[...]
    xn = _rms(x, g_ref[...]).astype(BF16)
    gu = _dot(xn, win_ref[...])
    gate = gu[:, :r]
    u = gu[:, r:]
    uc = cb_ref[...] + c0_ref[...] * cw_ref[0:1, :]
    uc = uc + c1_ref[...] * cw_ref[1:2, :]
    uc = uc + c2_ref[...] * cw_ref[2:3, :]
    uc = uc + u * cw_ref[3:4, :]
    uc_bf = uc.astype(BF16)
    rg = _band_matmul(uc_bf, wrg_ref, wins) + brg_ref[...]
    ig = _band_matmul(uc_bf, wig_ref, wins) + big_ref[...]
    a, xin = _lru_gates(uc, rg, ig, lam_ref[...])
    h = a * h0_ref[...] + xin
    hl_ref[...] = h
    u_ref[...] = u
    y = (h * jax.nn.gelu(gate)).astype(BF16)
    h_ref[...] = x + _dot(y, wout_ref[...])


def _rec_weights(w_in, conv_w, conv_b, w_rg, b_rg, w_ig, b_ig, lam, w_out):
    r = w_out.shape[0]
    tw = min(MXU_WIDTH, r)
    wins = _band_windows(r, w_rg.shape[0], tw)
    args = (w_in.astype(BF16), conv_w, conv_b.reshape(1, r), _band_pack(w_rg, wins, tw), b_rg.reshape(1, r),
            _band_pack(w_ig, wins, tw), b_ig.reshape(1, r), lam.reshape(1, r), w_out.astype(BF16))
    return wins, args


def _rec_prompt(x, g, wins, wargs):
    b, t, d = x.shape
    r = wargs[-1].shape[0]
    tc = _tile(t, 256)
    kern = functools.partial(_rec_prompt_kernel, wins=wins, tc=tc, r=r)
    wspecs = [_full(a.shape) for a in wargs]
    h, hl, cn = pl.pallas_call(
        kern,
        grid=(b, t // tc),
        in_specs=[pl.BlockSpec((1, tc, d), lambda i, j: (i, j, 0)), _full((1, d))] + wspecs,
        out_specs=[pl.BlockSpec((1, tc, d), lambda i, j: (i, j, 0)),
                   pl.BlockSpec((1, 1, r), lambda i, j: (i, 0, 0)),
                   pl.BlockSpec((1, CONV_W - 1, r), lambda i, j: (i, 0, 0))],
        out_shape=[jax.ShapeDtypeStruct((b, t, d), F32),
                   jax.ShapeDtypeStruct((b, 1, r), F32),
                   jax.ShapeDtypeStruct((b, CONV_W - 1, r), F32)],
        scratch_shapes=[pltpu.VMEM((SUBLANE + tc, r), F32), pltpu.VMEM((tc, r), F32),
                        pltpu.VMEM((tc, r), F32), pltpu.VMEM((tc, r), F32), pltpu.VMEM((SUBLANE, r), F32)],
        compiler_params=_cparams("arbitrary", "arbitrary"),
        name="rec_prompt",
    )(x, g.reshape(1, d), *wargs)
    return h, hl.reshape(b, r), cn


def _rec_sample(x, g, wins, wargs, conv0, h0):
    m, d = x.shape
    r = wargs[-1].shape[0]
    kern = functools.partial(_rec_sample_kernel, wins=wins, r=r)
    h, hl, u = pl.pallas_call(
        kern,
        out_shape=[jax.ShapeDtypeStruct((m, d), F32), jax.ShapeDtypeStruct((m, r), F32),
                   jax.ShapeDtypeStruct((m, r), F32)],
        compiler_params=pltpu.CompilerParams(vmem_limit_bytes=VMEM_LIMIT),
        name="rec_sample",
    )(x, g.reshape(1, d), *wargs, conv0[:, 0], conv0[:, 1], conv0[:, 2], h0)
    conv_new = jnp.stack([conv0[:, 1], conv0[:, 2], u], axis=1)
    return h, hl, conv_new


def _ple(h, p_ref, g_ref, wgate_ref, wproj_ref):
    gate = jax.nn.sigmoid(_dot(_rms(h, g_ref[...]).astype(BF16), wgate_ref[...]))
    return h + gate * _dot(p_ref[...].astype(BF16), wproj_ref[...])


def _ffn_dense_kernel(h_ref, g_ref, wgu_ref, wd_ref, p_ref, gp_ref, wpg_ref, wpp_ref,
                      o_ref, hn_s, acc_s):
    j = pl.program_id(1)
    tf = wd_ref.shape[0]

    @pl.when(j == 0)
    def _():
        hn_s[...] = _rms(h_ref[...], g_ref[...]).astype(BF16)
        acc_s[...] = jnp.zeros(acc_s.shape, F32)

    gu = _dot(hn_s[...], wgu_ref[0])
    act = (jax.nn.silu(gu[:, :tf]) * gu[:, tf:]).astype(BF16)
    acc_s[...] += _dot(act, wd_ref[...])

    @pl.when(j == pl.num_programs(1) - 1)
    def _():
        o_ref[...] = _ple(h_ref[...] + acc_s[...], p_ref, gp_ref, wpg_ref, wpp_ref)


def _gate_up_blocks(w_gu, tf):
    d, f2 = w_gu.shape
    nf = f2 // (2 * tf)
    return w_gu.reshape(d, 2, nf, tf).transpose(2, 0, 1, 3).reshape(nf, d, 2 * tf).astype(BF16)


def _ffn_dense(h, g, w_gu_blocks, w_down, p, g_ple, w_pgate, w_pproj):
    m, d = h.shape
    pd = p.shape[1]
    nf, _, tf2 = w_gu_blocks.shape
    tf = tf2 // 2
    tm = _tile(m, 512)
    return pl.pallas_call(
        _ffn_dense_kernel,
        grid=(m // tm, nf),
        in_specs=[pl.BlockSpec((tm, d), lambda i, j: (i, 0)), _full((1, d)),
                  pl.BlockSpec((1, d, tf2), lambda i, j: (j, 0, 0)),
                  pl.BlockSpec((tf, d), lambda i, j: (j, 0)),
                  pl.BlockSpec((tm, pd), lambda i, j: (i, 0)), _full((1, d)),
                  _full((d, d)), _full((pd, d))],
        out_specs=pl.BlockSpec((tm, d), lambda i, j: (i, 0)),
        out_shape=jax.ShapeDtypeStruct((m, d), F32),
        scratch_shapes=[pltpu.VMEM((tm, d), BF16), pltpu.VMEM((tm, d), F32)],
        compiler_params=_cparams("arbitrary", "arbitrary"),
        name="ffn_dense",
    )(h, g.reshape(1, d), w_gu_blocks, w_down, p, g_ple.reshape(1, d), w_pgate, w_pproj)


def _kvq_kernel(h_ref, gkv_ref, gq_ref, wkv_ref, bf_ref, wq_ref, k_ref, v_ref, q_ref, lf_ref,
                *, qk, nh, head_major):
    x = h_ref[...]
    xh = x * lax.rsqrt(jnp.mean(x * x, axis=-1, keepdims=True) + EPS)
    kvf = _dot((xh * gkv_ref[...]).astype(BF16), wkv_ref[...])
    k_ref[...] = kvf[:, :qk]
    v_ref[...] = kvf[:, qk:2 * qk]
    lf = jax.nn.log_sigmoid(kvf[:, 2 * qk:] + bf_ref[...])
    if head_major:
        lf_ref[0] = lf.T[:nh, :]
    else:
        lf_ref[...] = lf[:, :nh]
    q_ref[...] = _dot((xh * gq_ref[...]).astype(BF16), wq_ref[...]).astype(BF16)


def _kvq(h, g_kv, g_q, w_kv, b_f, w_q, *, seq_len=None):
    m, d = h.shape
    qk = w_q.shape[1]
    nh = b_f.shape[0]
    head_major = seq_len is not None
    tm = _tile(seq_len if head_major else m, 512, LANE if head_major else SUBLANE)
    wkv = jnp.pad(w_kv, ((0, 0), (0, LANE - nh))).astype(BF16)
    bfp = jnp.pad(b_f, (0, LANE - nh)).reshape(1, LANE)
    out_specs = [pl.BlockSpec((tm, qk), lambda i: (i, 0)), pl.BlockSpec((tm, qk), lambda i: (i, 0)),
                 pl.BlockSpec((tm, qk), lambda i: (i, 0))]
    out_shape = [jax.ShapeDtypeStruct((m, qk), F32), jax.ShapeDtypeStruct((m, qk), F32),
                 jax.ShapeDtypeStruct((m, qk), BF16)]
    if head_major:
        nt = seq_len // tm
        out_specs.append(pl.BlockSpec((1, nh, tm), lambda i: (i // nt, 0, i % nt)))
        out_shape.append(jax.ShapeDtypeStruct((m // seq_len, nh, seq_len), F32))
    else:
        out_specs.append(pl.BlockSpec((tm, nh), lambda i: (i, 0)))
        out_shape.append(jax.ShapeDtypeStruct((m, nh), F32))
    kern = functools.partial(_kvq_kernel, qk=qk, nh=nh, head_major=head_major)
    return pl.pallas_call(
        kern,
        grid=(m // tm,),
        in_specs=[pl.BlockSpec((tm, d), lambda i: (i, 0)), _full((1, d)), _full((1, d)),
                  _full(wkv.shape), _full((1, LANE)), _full((d, qk))],
        out_specs=out_specs,
        out_shape=out_shape,
        compiler_params=_cparams("arbitrary"),
        name="kvq",
    )(h, g_kv.reshape(1, d), g_q.reshape(1, d), wkv, bfp, w_q.astype(BF16))


def _cumsum_lanes(x):
    n = x.shape[1]
    lane = lax.broadcasted_iota(jnp.int32, x.shape, 1)
    k = 1
    while k < n:
        x = x + jnp.where(lane >= k, pltpu.roll(x, k, 1), 0.0)
        k *= 2
    return x


def _fold_lanes(x, op):
    r = x[:, :LANE]
    for c in range(1, x.shape[1] // LANE):
        r = op(r, x[:, c * LANE:(c + 1) * LANE])
    return r


def _attn_prompt_kernel(q_ref, k_ref, v_ref, lft_ref, o_ref, kb_s, vb_s, c2_s, s_s, p_s, *, tq, scale):
    h = pl.program_id(1)
    nblk = c2_s.shape[0]
    kb_s[...] = k_ref[0].astype(BF16)
    vb_s[...] = v_ref[0].astype(BF16)

    @pl.when(h == 0)
    def _():
        c = _cumsum_lanes(lft_ref[0]) * LOG2E
        for j in range(nblk):
            c2_s[j] = c[:, j * tq:(j + 1) * tq]

    c1 = scale * LOG2E
    row = lax.broadcasted_iota(jnp.int32, (tq, tq), 0)
    col = lax.broadcasted_iota(jnp.int32, (tq, tq), 1)
    causal = col <= row

    for n in range(nblk):
        q = q_ref[0, n * tq:(n + 1) * tq, :]
        base = n * (n + 1) // 2
        mrun = jnp.full((tq, LANE), NEG_BIG, F32)
        for kb in range(n + 1):
            s = lax.dot_general(q, kb_s[kb * tq:(kb + 1) * tq, :], (((1,), (1,)), ((), ())),
                                preferred_element_type=F32) * c1
            s = s - c2_s[kb, pl.ds(h, 1), :]
            if kb == n:
                s = jnp.where(causal, s, NEG_BIG)
            s_s[base + kb] = s
            mrun = jnp.maximum(mrun, _fold_lanes(s, jnp.maximum))
        m = jnp.max(mrun, axis=1, keepdims=True)
        lrun = jnp.zeros((tq, LANE), F32)
        for kb in range(n + 1):
            p = jnp.exp2(s_s[base + kb] - m)
            lrun = lrun + _fold_lanes(p, jnp.add)
            p_s[n, :, kb * tq:(kb + 1) * tq] = p.astype(BF16)
        acc = _dot(p_s[n, :, :(n + 1) * tq], vb_s[:(n + 1) * tq, :])
        o_ref[0, n * tq:(n + 1) * tq, :] = (acc / jnp.sum(lrun, axis=1, keepdims=True)).astype(BF16)


def _attn_prompt(q, k, v, lft, *, nh, scale):
    b, t, qk = q.shape
    hd = qk // nh
    tq = _tile(t, 256, LANE)
    nq = t // tq
    kern = functools.partial(_attn_prompt_kernel, tq=tq, scale=scale)
    head = pl.BlockSpec((1, t, hd), lambda i, h: (i, 0, h))
    return pl.pallas_call(
        kern,
        grid=(b, nh),
        in_specs=[head, head, head, pl.BlockSpec((1, nh, t), lambda i, h: (i, 0, 0))],
        out_specs=head,
        out_shape=jax.ShapeDtypeStruct((b, t, qk), BF16),
        scratch_shapes=[pltpu.VMEM((t, hd), BF16), pltpu.VMEM((t, hd), BF16),
                        pltpu.VMEM((nq, nh, tq), F32), pltpu.VMEM((nq * (nq + 1) // 2, tq, tq), F32),
                        pltpu.VMEM((nq, tq, t), BF16)],
        compiler_params=_cparams("arbitrary", "arbitrary"),
        name="attn_prompt",
    )(q, k, v, lft)


def _attn_sample_kernel(pt_ref, q_ref, kn_ref, vn_ref, lfn_ref, later_ref, *rest, pp, nh, scale):
    k_refs = rest[0:pp]
    v_refs = rest[pp:2 * pp]
    lf_refs = rest[2 * pp:3 * pp]
    o_ref = rest[3 * pp]
    m_s, l_s, acc_s, carry_s = rest[3 * pp + 1:]
    del pt_ref
    p = pl.program_id(1)
    q = q_ref[0].astype(BF16)

    @pl.when(p == 0)
    def _():
        kn = kn_ref[0].astype(BF16).astype(F32)
        s_new = jnp.sum(q.astype(F32) * kn, axis=1, keepdims=True) * scale
        m_s[...] = jnp.broadcast_to(s_new, m_s.shape)
        l_s[...] = jnp.ones(l_s.shape, F32)
        acc_s[...] = vn_ref[0]
        carry_s[...] = lfn_ref[0]

    scores = [lax.dot_general(q, k_refs[j][0].astype(BF16), (((1,), (1,)), ((), ())),
                              preferred_element_type=F32) for j in range(pp)]

    lfs = [lf_refs[j][0] for j in range(pp)]
    pieces = []
    for lf in lfs:
        pieces.extend(_split3_f32(lf))
    later = _dot(jnp.concatenate(pieces, axis=0).astype(BF16), later_ref[...])
    rows_per_page = later.shape[1]
    row = lax.broadcasted_iota(jnp.int32, (nh, rows_per_page), 0)
    lane = lax.broadcasted_iota(jnp.int32, (nh, rows_per_page), 1)
    own_head = jnp.bitwise_and(lane, nh - 1) == row

    carry = carry_s[...]
    probs = []
    for j in range(pp):
        bias = (later[3 * j * nh:(3 * j + 1) * nh] + later[(3 * j + 1) * nh:(3 * j + 2) * nh]
                + later[(3 * j + 2) * nh:(3 * j + 3) * nh]) + carry[:, 0:1]
        carry = carry + jnp.sum(lfs[j], axis=1, keepdims=True)
        s = jnp.where(own_head, scores[j] * scale + bias, NEG_BIG)
        m_j = jnp.max(s, axis=1, keepdims=True)
        pr = jnp.exp(s - m_j)
        probs.append((m_j, jnp.sum(pr, axis=1, keepdims=True), pr.astype(BF16)))
    carry_s[...] = carry
    parts = [(m_j, l_j, _dot(pr, v_refs[j][0].astype(BF16))) for j, (m_j, l_j, pr) in enumerate(probs)]

    m_old = m_s[...]
    m_new = m_old
    for m_j, _, _ in parts:
        m_new = jnp.maximum(m_new, m_j)
    alpha = jnp.exp(m_old - m_new)
    l_new = alpha * l_s[...]
    acc = alpha * acc_s[...]
    for m_j, l_j, o_j in parts:
        w_j = jnp.exp(m_j - m_new)
        l_new = l_new + w_j * l_j
        acc = acc + w_j * o_j
    m_s[...] = m_new
    l_s[...] = l_new
    acc_s[...] = acc

    @pl.when(p == pl.num_programs(1) - 1)
    def _():
        o_ref[0] = acc_s[...] / l_s[...]


def _attn_sample(q, k_new, v_new, lf_new, cache_k, cache_v, cache_logf, page_table, *, scale):
    b, qk = q.shape
    n_phys, ps, nh, hd = cache_k.shape
    assert ps == LANE and hd == LANE and nh == SUBLANE
    n_pages = page_table.shape[1]
    pp = 8
    while n_pages % pp:
        pp //= 2
    ck = cache_k.reshape(n_phys, ps * nh, hd)
    cv = cache_v.reshape(n_phys, ps * nh, hd)
    clf = jnp.swapaxes(cache_logf, 1, 2)
    pt = page_table.reshape(-1)
    lfn = jnp.broadcast_to(lf_new[:, :, None], (b, nh, LANE))
    later = (jnp.arange(ps)[:, None] > (jnp.arange(ps * nh) // nh)[None, :]).astype(BF16)

    def page_map(j):
        return lambda i, p, pt_ref: (pt_ref[i * n_pages + (n_pages - 1 - (p * pp + j))], 0, 0)

    head_spec = pl.BlockSpec((1, nh, hd), lambda i, p, pt_ref: (i, 0, 0))
    in_specs = [head_spec] * 4 + [pl.BlockSpec((ps, ps * nh), lambda i, p, pt_ref: (0, 0))]
    in_specs += [pl.BlockSpec((1, ps * nh, hd), page_map(j)) for j in range(pp)]
    in_specs += [pl.BlockSpec((1, ps * nh, hd), page_map(j)) for j in range(pp)]
    in_specs += [pl.BlockSpec((1, nh, ps), page_map(j)) for j in range(pp)]
    kern = functools.partial(_attn_sample_kernel, pp=pp, nh=nh, scale=scale)
    out = pl.pallas_call(
        kern,
        grid_spec=pltpu.PrefetchScalarGridSpec(
            num_scalar_prefetch=1,
            grid=(b, n_pages // pp),
            in_specs=in_specs,
            out_specs=head_spec,
            scratch_shapes=[pltpu.VMEM((nh, LANE), F32), pltpu.VMEM((nh, LANE), F32),
                            pltpu.VMEM((nh, hd), F32), pltpu.VMEM((nh, LANE), F32)]),
        out_shape=jax.ShapeDtypeStruct((b, nh, hd), F32),
        compiler_params=_cparams("arbitrary", "arbitrary"),
        name="attn_sample",
    )(pt, q.reshape(b, nh, hd), k_new.reshape(b, nh, hd), v_new.reshape(b, nh, hd), lfn, later,
      *([ck] * pp), *([cv] * pp), *([clf] * pp))
    return out.reshape(b, qk)


def _oproj_router_kernel(a_ref, h_ref, wo_ref, g_ref, wr_ref, br_ref, h4_ref, hn_ref, route_ref, cnt_ref,
                         carry_s, *, n_exp):
    i = pl.program_id(0)

    @pl.when(i == 0)
    def _():
        carry_s[...] = jnp.zeros(carry_s.shape, F32)

    h4 = h_ref[...] + _dot(a_ref[...].astype(BF16), wo_ref[...])
    h4_ref[...] = h4
    hn = _rms(h4, g_ref[...])
    hn_ref[...] = hn
    tm = hn.shape[0]

    x_hi, x_mid, _ = _split3(hn)
    w_hi, w_mid, _ = _split3(wr_ref[...])
    logits = _dot(x_hi, w_hi) + _dot(x_mid, w_hi) + _dot(x_hi, w_mid) + br_ref[...]

    lane = lax.broadcasted_iota(jnp.int32, (tm, LANE), 1)
    valid = lane < n_exp
    logits = jnp.where(valid, logits, NEG_BIG)
    ex = jnp.exp(logits - jnp.max(logits, axis=-1, keepdims=True))
    probs = jnp.where(valid, ex / jnp.sum(ex, axis=-1, keepdims=True), -1.0)
    p1 = jnp.max(probs, axis=-1, keepdims=True)
    i1 = jnp.min(jnp.where(probs == p1, lane, LANE), axis=-1, keepdims=True)
    probs2 = jnp.where(lane == i1, -1.0, probs)
    p2 = jnp.max(probs2, axis=-1, keepdims=True)
    i2 = jnp.min(jnp.where(probs2 == p2, lane, LANE), axis=-1, keepdims=True)
    den = p1 + p2
    w1 = p1 / den
    w2 = p2 / den

    onehot = ((lane == i1) | (lane == i2)).astype(F32)
    r_i = lax.broadcasted_iota(jnp.int32, (tm, tm), 0)
    c_i = lax.broadcasted_iota(jnp.int32, (tm, tm), 1)
    strict_lower = (c_i < r_i).astype(BF16)
    ranks = _dot(strict_lower, onehot.astype(BF16)) + carry_s[0:1, :]
    r1 = jnp.sum(jnp.where(lane == i1, ranks, 0.0), axis=-1, keepdims=True)
    r2 = jnp.sum(jnp.where(lane == i2, ranks, 0.0), axis=-1, keepdims=True)
    carry_s[0:1, :] = carry_s[0:1, :] + jnp.sum(onehot, axis=0, keepdims=True)
    cnt_ref[...] = jnp.broadcast_to(carry_s[0:1, :], cnt_ref.shape)

    route = jnp.where(lane == 0, i1.astype(F32), 0.0)
    route = jnp.where(lane == 1, i2.astype(F32), route)
    route = jnp.where(lane == 2, w1, route)
    route = jnp.where(lane == 3, w2, route)
    route = jnp.where(lane == 4, r1, route)
    route = jnp.where(lane == 5, r2, route)
    route_ref[...] = route


def _oproj_router(attn, h, w_o, g, w_router, b_router):
    m, d = h.shape
    qk = attn.shape[1]
    n_exp = w_router.shape[1]
    tm = _tile(m, 512)
    wr = jnp.pad(w_router, ((0, 0), (0, LANE - n_exp)))
    br = jnp.pad(b_router, (0, LANE - n_exp)).reshape(1, LANE)
    kern = functools.partial(_oproj_router_kernel, n_exp=n_exp)
    row = lambda w: pl.BlockSpec((tm, w), lambda i: (i, 0))
    return pl.pallas_call(
        kern,
        grid=(m // tm,),
        in_specs=[row(qk), row(d), _full((qk, d)), _full((1, d)), _full((d, LANE)), _full((1, LANE))],
        out_specs=[row(d), row(d), row(LANE), _full((SUBLANE, LANE))],
        out_shape=[jax.ShapeDtypeStruct((m, d), F32), jax.ShapeDtypeStruct((m, d), F32),
                   jax.ShapeDtypeStruct((m, LANE), F32), jax.ShapeDtypeStruct((SUBLANE, LANE), F32)],
        scratch_shapes=[pltpu.VMEM((SUBLANE, LANE), F32)],
        compiler_params=_cparams("arbitrary"),
        name="oproj_router",
    )(attn, h, w_o.astype(BF16), g.reshape(1, d), wr, br)


ISSUE_UNROLL = 8


def _row_copy_wait(src, dst, sem, rows):
    pltpu.make_async_copy(src.at[pl.ds(0, rows)], dst.at[pl.ds(0, rows)], sem).wait()


def _dispatch_kernel(pos_ref, hn_ref, xs_in_ref, xs_ref, sem, *, tm):
    del xs_in_ref
    i = pl.program_id(0)

    def body(r, c):
        for k in range(TOP_K):
            dst = pos_ref[(i * tm + r) * TOP_K + k]
            pltpu.make_async_copy(hn_ref.at[pl.ds(r, 1)], xs_ref.at[pl.ds(dst, 1)], sem).start()
        return c

    lax.fori_loop(0, tm, body, 0, unroll=ISSUE_UNROLL)
    for _ in range(TOP_K):
        _row_copy_wait(hn_ref, xs_ref, sem, tm)


def _dispatch(hn, pos, n_sorted):
    m, d = hn.shape
    tm = _tile(m, 512)
    kern = functools.partial(_dispatch_kernel, tm=tm)
    return pl.pallas_call(
        kern,
        grid_spec=pltpu.PrefetchScalarGridSpec(
            num_scalar_prefetch=1,
            grid=(m // tm,),
            in_specs=[pl.BlockSpec((tm, d), lambda i, pos_ref: (i, 0)),
                      pl.BlockSpec(memory_space=pl.ANY)],
            out_specs=pl.BlockSpec(memory_space=pl.ANY),
            scratch_shapes=[pltpu.SemaphoreType.DMA(())]),
        out_shape=jax.ShapeDtypeStruct((n_sorted, d), F32),
        input_output_aliases={2: 0},
        compiler_params=_cparams("arbitrary"),
        name="moe_dispatch",
    )(pos, hn, jnp.zeros((n_sorted, d), F32))


def _experts_kernel(te_ref, tv_ref, x_ref, wg_ref, wu_ref, wd_ref, y_ref, xb_s, acc_s):
    del te_ref
    i = pl.program_id(0)
    j = pl.program_id(1)
    valid = tv_ref[i] > 0

    @pl.when(valid & (j == 0))
    def _():
        xb_s[...] = x_ref[...].astype(BF16)
        acc_s[...] = jnp.zeros(acc_s.shape, F32)

    @pl.when(valid)
    def _():
        x = xb_s[...]
        gate = _dot(x, wg_ref[0].astype(BF16))
        up = _dot(x, wu_ref[0].astype(BF16))
        act = (jax.nn.silu(gate) * up).astype(BF16)
        acc_s[...] += _dot(act, wd_ref[0].astype(BF16))

    last = j == pl.num_programs(1) - 1

    @pl.when(valid & last)
    def _():
        y_ref[...] = acc_s[...]

    @pl.when(jnp.logical_not(valid) & last)
    def _():
        y_ref[...] = jnp.zeros(y_ref.shape, F32)


def _experts(xs, tile_expert, tile_valid, w_gu, w_down, tm):
    ns, d = xs.shape
    f = w_down.shape[1]
    tf = _tile(f, 512, LANE)
    nf = f // tf
    nt = ns // tm

    def jj(i, j, tv):
        return jnp.where(tv[i] > 0, j, nf - 1)

    return pl.pallas_call(
        _experts_kernel,
        grid_spec=pltpu.PrefetchScalarGridSpec(
            num_scalar_prefetch=2,
            grid=(nt, nf),
            in_specs=[pl.BlockSpec((tm, d), lambda i, j, te, tv: (i, 0)),
                      pl.BlockSpec((1, d, tf), lambda i, j, te, tv: (te[i], 0, jj(i, j, tv))),
                      pl.BlockSpec((1, d, tf), lambda i, j, te, tv: (te[i], 0, nf + jj(i, j, tv))),
                      pl.BlockSpec((1, tf, d), lambda i, j, te, tv: (te[i], jj(i, j, tv), 0))],
            out_specs=pl.BlockSpec((tm, d), lambda i, j, te, tv: (i, 0)),
            scratch_shapes=[pltpu.VMEM((tm, d), BF16), pltpu.VMEM((tm, d), F32)]),
        out_shape=jax.ShapeDtypeStruct((ns, d), F32),
        compiler_params=_cparams("arbitrary", "arbitrary"),
        name="moe_experts",
    )(tile_expert, tile_valid, xs, w_gu, w_gu, w_down)


def _combine_kernel(pos_ref, ys_ref, h_ref, route_ref, p_ref, gp_ref, wpg_ref, wpp_ref, gf_ref, o_ref,
                    buf_s, sem, *, tm):
    i = pl.program_id(0)
    n = pl.num_programs(0)

    def issue(step, slot):
        def body(r, c):
            for k in range(TOP_K):
                src = pos_ref[(step * tm + r) * TOP_K + k]
                pltpu.make_async_copy(ys_ref.at[pl.ds(src, 1)], buf_s.at[slot, k, pl.ds(r, 1)],
                                      sem.at[slot]).start()
            return c
        lax.fori_loop(0, tm, body, 0, unroll=ISSUE_UNROLL)

    @pl.when(i == 0)
    def _():
        issue(0, 0)

    slot = lax.rem(i, 2)

    @pl.when(i + 1 < n)
    def _():
        issue(i + 1, 1 - slot)

    for k in range(TOP_K):
        _row_copy_wait(ys_ref, buf_s.at[slot, k], sem.at[slot], tm)

    route = route_ref[...]
    h = h_ref[...] + route[:, 2:3] * buf_s[slot, 0] + route[:, 3:4] * buf_s[slot, 1]
    h = _ple(h, p_ref, gp_ref, wpg_ref, wpp_ref)
    o_ref[...] = _rms(h, gf_ref[...])


def _combine(ys, pos, h, route, p, g_ple, w_pgate, w_pproj, g_final):
    m, d = h.shape
    pd = p.shape[1]
    tm = _tile(m, 256)
    kern = functools.partial(_combine_kernel, tm=tm)
    row = lambda w: pl.BlockSpec((tm, w), lambda i, pos_ref: (i, 0))
    const = lambda s: pl.BlockSpec(s, lambda i, pos_ref: (0, 0))
    return pl.pallas_call(
        kern,
        grid_spec=pltpu.PrefetchScalarGridSpec(
            num_scalar_prefetch=1,
            grid=(m // tm,),
            in_specs=[pl.BlockSpec(memory_space=pl.ANY), row(d), row(LANE), row(pd), const((1, d)),
                      const((d, d)), const((pd, d)), const((1, d))],
            out_specs=row(d),
            scratch_shapes=[pltpu.VMEM((2, TOP_K, tm, d), F32), pltpu.SemaphoreType.DMA((2,))]),
        out_shape=jax.ShapeDtypeStruct((m, d), F32),
        compiler_params=_cparams("arbitrary"),
        name="moe_combine",
    )(pos, ys, h, route, p, g_ple.reshape(1, d), w_pgate, w_pproj, g_final.reshape(1, d))


def _route_plan(route, counts, n_exp, tm, n_tiles):
    e = route[:, 0:TOP_K].astype(jnp.int32)
    rank = route[:, 4:4 + TOP_K].astype(jnp.int32)
    cnt = counts[0, :n_exp].astype(jnp.int32)
    ntile = (cnt + tm - 1) // tm
    tile_end = jnp.cumsum(ntile)
    row_start = (tile_end - ntile) * tm
    sel = e[:, :, None] == jnp.arange(n_exp)[None, None, :]
    pos = jnp.sum(jnp.where(sel, row_start[None, None, :], 0), axis=-1) + rank
    tiles = jnp.arange(n_tiles)
    total = tile_end[-1]
    te = jnp.minimum(jnp.sum(tiles[:, None] >= tile_end[None, :], axis=1), n_exp - 1)
    valid = tiles < total
    te_last = jnp.sum(jnp.where(tiles == total - 1, te, 0))
    te = jnp.where(valid, te, te_last).astype(jnp.int32)
    return pos.reshape(-1).astype(jnp.int32), te, valid.astype(jnp.int32)


def _moe_and_final(attn, h, w_o, g_ffn, w_router, b_router, w_gu, w_down, p, g_ple, w_pgate, w_pproj,
                   g_final, *, tm_expert):
    m, d = h.shape
    n_exp = w_router.shape[1]
    h4, hn, route, counts = _oproj_router(attn, h, w_o, g_ffn, w_router, b_router)
    n_tiles = -(-(m * TOP_K) // tm_expert) + n_exp
    pos, te, tv = _route_plan(route, counts, n_exp, tm_expert, n_tiles)
    xs = _dispatch(hn, pos, n_tiles * tm_expert)
    ys = _experts(xs, te, tv, w_gu, w_down, tm_expert)
    return _combine(ys, pos, h4, route, p, g_ple, w_pgate, w_pproj, g_final)


def kernel(x_prompt, x_sample, state_h, state_conv, cache_k, cache_v, cache_logf, page_table, p_prompt, p_sample, g_mix_a, w_in_a, conv_w_a, conv_b_a, w_rg_a, b_rg_a, w_ig_a, b_ig_a, lru_lambda_a, w_out_a, g_kv, w_kv, b_f, g_mix_b, w_q_b, w_o_b, g_ffn, w_gu_dense, w_down_dense, w_router, b_router, w_gu_moe, w_down_moe, g_ple, w_ple_gate, w_ple_proj, g_final):
    assert w_in_a.shape[0] == 1 and w_q_b.shape[0] == 1 and g_ffn.shape[0] == 2
    assert x_sample.shape[1] == 1
    bsz, t_len, d = x_prompt.shape
    dec_b = x_sample.shape[0]
    nh = b_f.shape[0]
    qk = w_q_b.shape[2]
    hd = qk // nh
    r = w_out_a.shape[1]
    scale = hd ** -0.5
    mp = bsz * t_len

    wins, rec_w = _rec_weights(w_in_a[0], conv_w_a[0], conv_b_a[0], w_rg_a[0], b_rg_a[0], w_ig_a[0],
                               b_ig_a[0], lru_lambda_a[0], w_out_a[0])
    w_gu_d = _gate_up_blocks(w_gu_dense[0], _tile(w_down_dense.shape[1], 512, LANE))
    w_down_d = w_down_dense[0].astype(BF16)
    w_pg = [w_ple_gate[l].astype(BF16) for l in range(2)]
    w_pp = [w_ple_proj[l].astype(BF16) for l in range(2)]

    def rest_of_step(h1, p, seq_len):
        h3 = _ffn_dense(h1, g_ffn[0], w_gu_d, w_down_d, p[0], g_ple[0], w_pg[0], w_pp[0])
        return h3, _kvq(h3, g_kv, g_mix_b[0], w_kv, b_f, w_q_b[0], seq_len=seq_len)

    def moe(attn, h3, p, tm_expert):
        return _moe_and_final(attn, h3, w_o_b[0], g_ffn[1], w_router[0], b_router[0], w_gu_moe[0],
                              w_down_moe[0], p[1], g_ple[1], w_pg[1], w_pp[1], g_final, tm_expert=tm_expert)

    h1, hl_p, cn_p = _rec_prompt(x_prompt, g_mix_a[0], wins, rec_w)
    pp = p_prompt.reshape(2, mp, -1)
    h3, (k_p, v_p, q_p, lft_p) = rest_of_step(h1.reshape(mp, d), pp, t_len)
    attn_p = _attn_prompt(q_p.reshape(bsz, t_len, qk), k_p.reshape(bsz, t_len, qk), v_p.reshape(bsz, t_len, qk),
                          lft_p, nh=nh, scale=scale)
    lf_p = jnp.swapaxes(lft_p, 1, 2)
    y_p = moe(attn_p.reshape(mp, qk), h3, pp, min(1024, mp))

    h1s, hl_s, cn_s = _rec_sample(x_sample.reshape(dec_b, d), g_mix_a[0], wins, rec_w, state_conv[0], state_h[0])
    ps = p_sample.reshape(2, dec_b, -1)
    h3s, (k_s, v_s, q_s, lf_s) = rest_of_step(h1s, ps, None)
    attn_s = _attn_sample(q_s.astype(F32), k_s, v_s, lf_s, cache_k, cache_v, cache_logf, page_table, scale=scale)
    y_s = moe(attn_s, h3s, ps, dec_b)

    return (y_p.reshape(bsz, t_len, d), y_s.reshape(dec_b, 1, d),
            hl_p[None], cn_p[None],
            k_p.reshape(bsz, t_len, nh, hd), v_p.reshape(bsz, t_len, nh, hd), lf_p,
            hl_s[None], cn_s[None],
            k_s.reshape(dec_b, 1, nh, hd), v_s.reshape(dec_b, 1, nh, hd), lf_s.reshape(dec_b, 1, nh))
```

```python
import functools

import jax
import jax.numpy as jnp
from jax import lax
from jax.experimental import pallas as pl
from jax.experimental.pallas import tpu as pltpu

F32 = jnp.float32
BF16 = jnp.bfloat16

EPS = 1e-6
LRU_C = 8.0
N_LRU_BLOCKS = 8
CONV_W = 4
N_EXPERTS = 8
TOP_K = 2

LANE = 128
SUBLANE = 8
MXU_WIDTH = 256
VMEM_BYTES = 64 * 1024 * 1024
VMEM_LIMIT = VMEM_BYTES * 7 // 8
NEG_BIG = -1e30
LOG2E = 1.4426950408889634


def _cparams(*sem):
    return pltpu.CompilerParams(dimension_semantics=sem, vmem_limit_bytes=VMEM_LIMIT)


def _tile(n, pref, align=SUBLANE):
    if n <= pref:
        return n
    t = pref // align * align
    while n % t:
        t -= align
    return t


def _full(shape):
    nd = len(shape)
    return pl.BlockSpec(shape, lambda *_: (0,) * nd)


def _rms(x, g):
    ms = jnp.mean(x * x, axis=-1, keepdims=True)
    return x * lax.rsqrt(ms + EPS) * g


def _dot(a, b):
    return jnp.dot(a, b, preferred_element_type=F32)


def _split3_f32(x):
    hi = x.astype(BF16).astype(F32)
    r1 = x - hi
    mid = r1.astype(BF16).astype(F32)
    lo = (r1 - mid).astype(BF16).astype(F32)
    return hi, mid, lo


def _split3(x):
    return tuple(piece.astype(BF16) for piece in _split3_f32(x))


def _band_windows(r, nblk, tw):
    blk = r // nblk
    wins = []
    c0 = 0
    while c0 < r:
        c1 = min(c0 + tw, r)
        b0, b1 = c0 // blk, (c1 - 1) // blk
        k0 = (b0 * blk) // LANE * LANE
        k1 = min(r, -(-((b1 + 1) * blk) // LANE) * LANE)
        wins.append((c0, c1, k0, k1))
        c0 = c1
    return tuple(wins)


def _band_pack(ws, wins, tw):
    parts = []
    dense = [jax.scipy.linalg.block_diag(*w) for w in ws]
    for c0, c1, k0, k1 in wins:
        parts.append(jnp.concatenate(
            [jnp.pad(dn[k0:k1, c0:c1], ((0, 0), (0, tw - (c1 - c0)))) for dn in dense], axis=1))
    return jnp.concatenate(parts, axis=0).astype(BF16)


def _band_matmul(x_bf, w_ref, wins):
    tw = w_ref.shape[1] // 2
    results = []
    for which in range(2):
        outs = []
        off = 0
        for c0, c1, k0, k1 in wins:
            o = _dot(x_bf[:, k0:k1], w_ref[off:off + (k1 - k0), which * tw:(which + 1) * tw])
            outs.append(o[:, :c1 - c0])
            off += k1 - k0
        results.append(jnp.concatenate(outs, axis=1) if len(outs) > 1 else outs[0])
    return tuple(results)


def _lru_gates(uc, rg, ig, lam):
    r = jax.nn.sigmoid(rg)
    i = jax.nn.sigmoid(ig)
    log_a = (-LRU_C) * r * jax.nn.softplus(-lam)
    a = jnp.exp(log_a)
    one_minus_a2 = jnp.tanh(-log_a) * (a * a + 1.0)
    return a, jnp.sqrt(one_minus_a2) * (i * uc)


def _rec_prompt_kernel(x_ref, g_ref, win_ref, cw_ref, cb_ref, wband_ref, brg_ref, big_ref,
                       lam_ref, wout_ref, h_ref, hl_ref, cn_ref, tail_s, a_s, xi_s, hs_s, hc_s,
                       *, wins, tc, r):
    t = pl.program_id(1)

    @pl.when(t == 0)
    def _():
        tail_s[...] = jnp.zeros(tail_s.shape, F32)
        hc_s[...] = jnp.zeros(hc_s.shape, F32)

    x = x_ref[0]
    xn = _rms(x, g_ref[...]).astype(BF16)
    gu = _dot(xn, win_ref[...])
    gate = gu[:, :r]
    u = gu[:, r:]

    tail = tail_s[...]
    row8 = lax.broadcasted_iota(jnp.int32, (SUBLANE, r), 0)
    uc = cb_ref[...]
    for j in range(CONV_W):
        k = CONV_W - 1 - j
        if k:
            sh = pltpu.roll(u, k, 0)
            head = jnp.where(row8 < k, pltpu.roll(tail, k, 0), sh[0:SUBLANE])
            sh = jnp.concatenate([head, sh[SUBLANE:]], axis=0)
        else:
            sh = u
        uc = uc + sh * cw_ref[j:j + 1, :]
    tail_s[...] = u[tc - SUBLANE:tc]

    @pl.when(t == pl.num_programs(1) - 1)
    def _():
        cn_ref[0] = u[tc - (CONV_W - 1):tc]

    rg, ig = _band_matmul(uc.astype(BF16), wband_ref, wins)
    a, xin = _lru_gates(uc, rg + brg_ref[...], ig + big_ref[...], lam_ref[...])
    a_s[...] = a
    xi_s[...] = xin

    row = lax.broadcasted_iota(jnp.int32, (SUBLANE, r), 0)

    def body(gi, h_in):
        off = pl.multiple_of(gi * SUBLANE, SUBLANE)
        a8 = a_s[pl.ds(off, SUBLANE), :]
        x8 = xi_s[pl.ds(off, SUBLANE), :]
        for k in (1, 2, 4):
            m = row >= k
            x8 = x8 + a8 * jnp.where(m, pltpu.roll(x8, k, 0), 0.0)
            a8 = a8 * jnp.where(m, pltpu.roll(a8, k, 0), 1.0)
        h8 = a8 * h_in + x8
        hs_s[pl.ds(off, SUBLANE), :] = h8
        return h8[SUBLANE - 1:SUBLANE, :]

    h_last = lax.fori_loop(0, tc // SUBLANE, body, hc_s[0:1, :])
    hc_s[0:1, :] = h_last
    hl_ref[0] = h_last

    y = (hs_s[...] * jax.nn.gelu(gate)).astype(BF16)
    h_ref[0] = x + _dot(y, wout_ref[...])


def _rec_sample_kernel(x_ref, g_ref, win_ref, cw_ref, cb_ref, wband_ref, brg_ref, big_ref,
                       lam_ref, wout_ref, c0_ref, c1_ref, c2_ref, h0_ref, h_ref, hl_ref, u_ref,
                       *, wins, r):
    x = x_ref[...]
    xn = _rms(x, g_ref[...]).astype(BF16)
    gu = _dot(xn, win_ref[...])
    gate = gu[:, :r]
    u = gu[:, r:]
    uc = cb_ref[...] + c0_ref[...] * cw_ref[0:1, :]
    uc = uc + c1_ref[...] * cw_ref[1:2, :]
    uc = uc + c2_ref[...] * cw_ref[2:3, :]
    uc = uc + u * cw_ref[3:4, :]
    rg, ig = _band_matmul(uc.astype(BF16), wband_ref, wins)
    a, xin = _lru_gates(uc, rg + brg_ref[...], ig + big_ref[...], lam_ref[...])
    h = a * h0_ref[...] + xin
    hl_ref[...] = h
    u_ref[...] = u
    y = (h * jax.nn.gelu(gate)).astype(BF16)
    h_ref[...] = x + _dot(y, wout_ref[...])


def _rec_weights(w_in, conv_w, conv_b, w_rg, b_rg, w_ig, b_ig, lam, w_out):
    r = w_out.shape[0]
    tw = min(MXU_WIDTH, r)
    wins = _band_windows(r, w_rg.shape[0], tw)
    args = (w_in.astype(BF16), conv_w, conv_b.reshape(1, r), _band_pack((w_rg, w_ig), wins, tw),
            b_rg.reshape(1, r), b_ig.reshape(1, r), lam.reshape(1, r), w_out.astype(BF16))
    return wins, args


def _rec_prompt(x, g, wins, wargs):
    b, t, d = x.shape
    r = wargs[-1].shape[0]
    tc = _tile(t, 256)
    kern = functools.partial(_rec_prompt_kernel, wins=wins, tc=tc, r=r)
    wspecs = [_full(a.shape) for a in wargs]
    h, hl, cn = pl.pallas_call(
        kern,
        grid=(b, t // tc),
        in_specs=[pl.BlockSpec((1, tc, d), lambda i, j: (i, j, 0)), _full((1, d))] + wspecs,
        out_specs=[pl.BlockSpec((1, tc, d), lambda i, j: (i, j, 0)),
                   pl.BlockSpec((1, 1, r), lambda i, j: (i, 0, 0)),
                   pl.BlockSpec((1, CONV_W - 1, r), lambda i, j: (i, 0, 0))],
        out_shape=[jax.ShapeDtypeStruct((b, t, d), F32),
                   jax.ShapeDtypeStruct((b, 1, r), F32),
                   jax.ShapeDtypeStruct((b, CONV_W - 1, r), F32)],
        scratch_shapes=[pltpu.VMEM((SUBLANE, r), F32), pltpu.VMEM((tc, r), F32),
                        pltpu.VMEM((tc, r), F32), pltpu.VMEM((tc, r), F32), pltpu.VMEM((SUBLANE, r), F32)],
        compiler_params=_cparams("arbitrary", "arbitrary"),
        name="rec_prompt",
    )(x, g.reshape(1, d), *wargs)
    return h, hl.reshape(b, r), cn


def _rec_sample(x, g, wins, wargs, conv0, h0):
    m, d = x.shape
    r = wargs[-1].shape[0]
    kern = functools.partial(_rec_sample_kernel, wins=wins, r=r)
    h, hl, u = pl.pallas_call(
        kern,
        out_shape=[jax.ShapeDtypeStruct((m, d), F32), jax.ShapeDtypeStruct((m, r), F32),
                   jax.ShapeDtypeStruct((m, r), F32)],
        compiler_params=pltpu.CompilerParams(vmem_limit_bytes=VMEM_LIMIT),
        name="rec_sample",
    )(x, g.reshape(1, d), *wargs, conv0[:, 0], conv0[:, 1], conv0[:, 2], h0)
    conv_new = jnp.stack([conv0[:, 1], conv0[:, 2], u], axis=1)
    return h, hl, conv_new


def _ple(h, p_ref, g_ref, wgate_ref, wproj_ref):
    gate = jax.nn.sigmoid(_dot(_rms(h, g_ref[...]).astype(BF16), wgate_ref[...]))
    return h + gate * _dot(p_ref[...].astype(BF16), wproj_ref[...])


def _ffn_dense_kernel(h_ref, g_ref, wg_ref, wu_ref, wd_ref, p_ref, gp_ref, wpg_ref, wpp_ref,
                      o_ref, hn_s, acc_s):
    j = pl.program_id(1)

    @pl.when(j == 0)
    def _():
        hn_s[...] = _rms(h_ref[...], g_ref[...]).astype(BF16)
        acc_s[...] = jnp.zeros(acc_s.shape, F32)

    hn = hn_s[...]
    act = (jax.nn.silu(_dot(hn, wg_ref[...])) * _dot(hn, wu_ref[...])).astype(BF16)
    acc_s[...] += _dot(act, wd_ref[...])

    @pl.when(j == pl.num_programs(1) - 1)
    def _():
        o_ref[...] = _ple(h_ref[...] + acc_s[...], p_ref, gp_ref, wpg_ref, wpp_ref)


def _ffn_dense(h, g, w_gu, w_down, p_layers, layer, g_ple, w_pgate, w_pproj):
    m, d = h.shape
    f = w_down.shape[0]
    pd = p_layers.shape[2]
    tm = _tile(m, 512)
    tf = _tile(f, 512, LANE)
    nf = f // tf
    return pl.pallas_call(
        _ffn_dense_kernel,
        grid=(m // tm, nf),
        in_specs=[pl.BlockSpec((tm, d), lambda i, j: (i, 0)), _full((1, d)),
                  pl.BlockSpec((d, tf), lambda i, j: (0, j)),
                  pl.BlockSpec((d, tf), lambda i, j: (0, nf + j)),
                  pl.BlockSpec((tf, d), lambda i, j: (j, 0)),
                  pl.BlockSpec((None, tm, pd), lambda i, j: (layer, i, 0)), _full((1, d)),
                  _full((d, d)), _full((pd, d))],
        out_specs=pl.BlockSpec((tm, d), lambda i, j: (i, 0)),
        out_shape=jax.ShapeDtypeStruct((m, d), F32),
        scratch_shapes=[pltpu.VMEM((tm, d), BF16), pltpu.VMEM((tm, d), F32)],
        compiler_params=_cparams("arbitrary", "arbitrary"),
        name="ffn_dense",
    )(h, g.reshape(1, d), w_gu, w_gu, w_down, p_layers, g_ple.reshape(1, d), w_pgate, w_pproj)


def _kvq_kernel(h_ref, gkv_ref, gq_ref, wkv_ref, bf_ref, wq_ref, k_ref, v_ref, q_ref, lf_ref,
                *, qk, nh, head_major):
    x = h_ref[...]
    xh = x * lax.rsqrt(jnp.mean(x * x, axis=-1, keepdims=True) + EPS)
    kvf = _dot((xh * gkv_ref[...]).astype(BF16), wkv_ref[...])
    k_ref[...] = kvf[:, :qk]
    v_ref[...] = kvf[:, qk:2 * qk]
    lf = jax.nn.log_sigmoid(kvf[:, 2 * qk:] + bf_ref[...])
    if head_major:
        lf_ref[0] = lf.T[:nh, :]
    else:
        lf_ref[...] = lf[:, :nh]
    q_ref[...] = _dot((xh * gq_ref[...]).astype(BF16), wq_ref[...]).astype(BF16)


def _kvq(h, g_kv, g_q, w_kv, b_f, w_q, *, seq_len=None):
    m, d = h.shape
    qk = w_q.shape[1]
    nh = b_f.shape[0]
    head_major = seq_len is not None
    tm = _tile(seq_len if head_major else m, 512, LANE if head_major else SUBLANE)
    wkv = jnp.pad(w_kv, ((0, 0), (0, LANE - nh))).astype(BF16)
    bfp = jnp.pad(b_f, (0, LANE - nh)).reshape(1, LANE)
    out_specs = [pl.BlockSpec((tm, qk), lambda i: (i, 0)), pl.BlockSpec((tm, qk), lambda i: (i, 0)),
                 pl.BlockSpec((tm, qk), lambda i: (i, 0))]
    out_shape = [jax.ShapeDtypeStruct((m, qk), F32), jax.ShapeDtypeStruct((m, qk), F32),
                 jax.ShapeDtypeStruct((m, qk), BF16)]
    if head_major:
        nt = seq_len // tm
        out_specs.append(pl.BlockSpec((1, nh, tm), lambda i: (i // nt, 0, i % nt)))
        out_shape.append(jax.ShapeDtypeStruct((m // seq_len, nh, seq_len), F32))
    else:
        out_specs.append(pl.BlockSpec((tm, nh), lambda i: (i, 0)))
        out_shape.append(jax.ShapeDtypeStruct((m, nh), F32))
    kern = functools.partial(_kvq_kernel, qk=qk, nh=nh, head_major=head_major)
    return pl.pallas_call(
        kern,
        grid=(m // tm,),
        in_specs=[pl.BlockSpec((tm, d), lambda i: (i, 0)), _full((1, d)), _full((1, d)),
                  _full(wkv.shape), _full((1, LANE)), _full((d, qk))],
        out_specs=out_specs,
        out_shape=out_shape,
        compiler_params=_cparams("arbitrary"),
        name="kvq",
    )(h, g_kv.reshape(1, d), g_q.reshape(1, d), wkv, bfp, w_q.astype(BF16))


def _cumsum_lanes(x):
    n = x.shape[1]
    lane = lax.broadcasted_iota(jnp.int32, x.shape, 1)
    k = 1
    while k < n:
        x = x + jnp.where(lane >= k, pltpu.roll(x, k, 1), 0.0)
        k *= 2
    return x


def _fold_lanes(x, op):
    r = x[:, :LANE]
    for c in range(1, x.shape[1] // LANE):
        r = op(r, x[:, c * LANE:(c + 1) * LANE])
    return r


def _attn_prompt_kernel(q_ref, k_ref, v_ref, lft_ref, o_ref, kb_s, vb_s, c2_s, s_s, p_s, *, tq, scale):
    h = pl.program_id(1)
    nblk = c2_s.shape[0]
    kb_s[...] = k_ref[0].astype(BF16)
    vb_s[...] = v_ref[0].astype(BF16)

    @pl.when(h == 0)
    def _():
        c = _cumsum_lanes(lft_ref[0]) * LOG2E
        for j in range(nblk):
            c2_s[j] = c[:, j * tq:(j + 1) * tq]

    c1 = scale * LOG2E
    row = lax.broadcasted_iota(jnp.int32, (tq, tq), 0)
    col = lax.broadcasted_iota(jnp.int32, (tq, tq), 1)
    causal = col <= row

    for n in range(nblk):
        q = q_ref[0, n * tq:(n + 1) * tq, :]
        base = n * (n + 1) // 2
        mrun = jnp.full((tq, LANE), NEG_BIG, F32)
        for kb in range(n + 1):
            s = lax.dot_general(q, kb_s[kb * tq:(kb + 1) * tq, :], (((1,), (1,)), ((), ())),
                                preferred_element_type=F32) * c1
            s = s - c2_s[kb, pl.ds(h, 1), :]
            if kb == n:
                s = jnp.where(causal, s, NEG_BIG)
            s_s[base + kb] = s
            mrun = jnp.maximum(mrun, _fold_lanes(s, jnp.maximum))
        m = jnp.max(mrun, axis=1, keepdims=True)
        lrun = jnp.zeros((tq, LANE), F32)
        for kb in range(n + 1):
            p = jnp.exp2(s_s[base + kb] - m)
            lrun = lrun + _fold_lanes(p, jnp.add)
            p_s[n, :, kb * tq:(kb + 1) * tq] = p.astype(BF16)
        acc = _dot(p_s[n, :, :(n + 1) * tq], vb_s[:(n + 1) * tq, :])
        o_ref[0, n * tq:(n + 1) * tq, :] = (acc / jnp.sum(lrun, axis=1, keepdims=True)).astype(BF16)


def _attn_prompt(q, k, v, lft, *, nh, scale):
    b, t, qk = q.shape
    hd = qk // nh
    tq = _tile(t, 256, LANE)
    nq = t // tq
    kern = functools.partial(_attn_prompt_kernel, tq=tq, scale=scale)
    head = pl.BlockSpec((1, t, hd), lambda i, h: (i, 0, h))
    return pl.pallas_call(
        kern,
        grid=(b, nh),
        in_specs=[head, head, head, pl.BlockSpec((1, nh, t), lambda i, h: (i, 0, 0))],
        out_specs=head,
        out_shape=jax.ShapeDtypeStruct((b, t, qk), BF16),
        scratch_shapes=[pltpu.VMEM((t, hd), BF16), pltpu.VMEM((t, hd), BF16),
                        pltpu.VMEM((nq, nh, tq), F32), pltpu.VMEM((nq * (nq + 1) // 2, tq, tq), F32),
                        pltpu.VMEM((nq, tq, t), BF16)],
        compiler_params=_cparams("arbitrary", "arbitrary"),
        name="attn_prompt",
    )(q, k, v, lft)


def _attn_sample_kernel(pt_ref, q_ref, kn_ref, vn_ref, lfn_ref, later_ref, *rest, pp, nh, scale):
    k_refs = rest[0:pp]
    v_refs = rest[pp:2 * pp]
    lf_refs = rest[2 * pp:3 * pp]
    o_ref = rest[3 * pp]
    m_s, l_s, acc_s, carry_s = rest[3 * pp + 1:]
    del pt_ref
    p = pl.program_id(1)
    q = q_ref[0].astype(BF16)

    @pl.when(p == 0)
    def _():
        kn = kn_ref[0].astype(BF16).astype(F32)
        s_new = jnp.sum(q.astype(F32) * kn, axis=1, keepdims=True) * scale
        m_s[...] = jnp.broadcast_to(s_new, m_s.shape)
        l_s[...] = jnp.ones(l_s.shape, F32)
        acc_s[...] = vn_ref[0]
        carry_s[...] = lfn_ref[0]

    scores = [lax.dot_general(q, k_refs[j][0].astype(BF16), (((1,), (1,)), ((), ())),
                              preferred_element_type=F32) for j in range(pp)]

    lfs = [lf_refs[j][0] for j in range(pp)]
    pieces = []
    for lf in lfs:
        pieces.extend(_split3_f32(lf))
    later = _dot(jnp.concatenate(pieces, axis=0).astype(BF16), later_ref[...])
    rows_per_page = later.shape[1]
    row = lax.broadcasted_iota(jnp.int32, (nh, rows_per_page), 0)
    lane = lax.broadcasted_iota(jnp.int32, (nh, rows_per_page), 1)
    own_head = jnp.bitwise_and(lane, nh - 1) == row

    carry = carry_s[...]
    probs = []
    for j in range(pp):
        bias = (later[3 * j * nh:(3 * j + 1) * nh] + later[(3 * j + 1) * nh:(3 * j + 2) * nh]
                + later[(3 * j + 2) * nh:(3 * j + 3) * nh]) + carry[:, 0:1]
        carry = carry + jnp.sum(lfs[j], axis=1, keepdims=True)
        s = jnp.where(own_head, scores[j] * scale + bias, NEG_BIG)
        m_j = jnp.max(s, axis=1, keepdims=True)
        pr = jnp.exp(s - m_j)
        probs.append((m_j, jnp.sum(pr, axis=1, keepdims=True), pr.astype(BF16)))
    carry_s[...] = carry
    parts = [(m_j, l_j, _dot(pr, v_refs[j][0].astype(BF16))) for j, (m_j, l_j, pr) in enumerate(probs)]

    m_old = m_s[...]
    m_new = m_old
    for m_j, _, _ in parts:
        m_new = jnp.maximum(m_new, m_j)
    alpha = jnp.exp(m_old - m_new)
    l_new = alpha * l_s[...]
    acc = alpha * acc_s[...]
    for m_j, l_j, o_j in parts:
        w_j = jnp.exp(m_j - m_new)
        l_new = l_new + w_j * l_j
        acc = acc + w_j * o_j
    m_s[...] = m_new
    l_s[...] = l_new
    acc_s[...] = acc

    @pl.when(p == pl.num_programs(1) - 1)
    def _():
        o_ref[0] = acc_s[...] / l_s[...]


def _attn_sample(q, k_new, v_new, lf_new, cache_k, cache_v, cache_logf, page_table, *, scale):
    b, qk = q.shape
    n_phys, ps, nh, hd = cache_k.shape
    assert ps == LANE and hd == LANE and nh == SUBLANE
    n_pages = page_table.shape[1]
    pp = 8
    while n_pages % pp:
        pp //= 2
    ck = cache_k.reshape(n_phys, ps * nh, hd)
    cv = cache_v.reshape(n_phys, ps * nh, hd)
    clf = jnp.swapaxes(cache_logf, 1, 2)
    pt = page_table.reshape(-1)
    lfn = jnp.broadcast_to(lf_new[:, :, None], (b, nh, LANE))
    later = (jnp.arange(ps)[:, None] > (jnp.arange(ps * nh) // nh)[None, :]).astype(BF16)

    def page_map(j):
        return lambda i, p, pt_ref: (pt_ref[i * n_pages + (n_pages - 1 - (p * pp + j))], 0, 0)

    head_spec = pl.BlockSpec((1, nh, hd), lambda i, p, pt_ref: (i, 0, 0))
    in_specs = [head_spec] * 4 + [pl.BlockSpec((ps, ps * nh), lambda i, p, pt_ref: (0, 0))]
    in_specs += [pl.BlockSpec((1, ps * nh, hd), page_map(j)) for j in range(pp)]
    in_specs += [pl.BlockSpec((1, ps * nh, hd), page_map(j)) for j in range(pp)]
    in_specs += [pl.BlockSpec((1, nh, ps), page_map(j)) for j in range(pp)]
    kern = functools.partial(_attn_sample_kernel, pp=pp, nh=nh, scale=scale)
    out = pl.pallas_call(
        kern,
        grid_spec=pltpu.PrefetchScalarGridSpec(
            num_scalar_prefetch=1,
            grid=(b, n_pages // pp),
            in_specs=in_specs,
            out_specs=head_spec,
            scratch_shapes=[pltpu.VMEM((nh, LANE), F32), pltpu.VMEM((nh, LANE), F32),
                            pltpu.VMEM((nh, hd), F32), pltpu.VMEM((nh, LANE), F32)]),
        out_shape=jax.ShapeDtypeStruct((b, nh, hd), F32),
        compiler_params=_cparams("arbitrary", "arbitrary"),
        name="attn_sample",
    )(pt, q.reshape(b, nh, hd), k_new.reshape(b, nh, hd), v_new.reshape(b, nh, hd), lfn, later,
      *([ck] * pp), *([cv] * pp), *([clf] * pp))
    return out.reshape(b, qk)


def _oproj_router_kernel(a_ref, h_ref, wo_ref, g_ref, wr_ref, br_ref, h4_ref, hn_ref, route_ref, cnt_ref,
                         carry_s, *, n_exp):
    i = pl.program_id(0)

    @pl.when(i == 0)
    def _():
        carry_s[...] = jnp.zeros(carry_s.shape, F32)

    h4 = h_ref[...] + _dot(a_ref[...].astype(BF16), wo_ref[...])
    h4_ref[...] = h4
    hn = _rms(h4, g_ref[...])
    hn_ref[...] = hn
    tm = hn.shape[0]

    def dot_nt(a, b):
        return lax.dot_general(a, b, (((1,), (1,)), ((), ())), preferred_element_type=F32)

    x_hi, x_mid, _ = _split3(hn)
    w_hi, w_mid, _ = _split3(wr_ref[...])
    logits = dot_nt(w_hi, x_hi) + dot_nt(w_hi, x_mid) + dot_nt(w_mid, x_hi) + br_ref[:, 0:1]

    exp_id = lax.broadcasted_iota(jnp.int32, (n_exp, tm), 0)
    ex = jnp.exp(logits - jnp.max(logits, axis=0, keepdims=True))
    probs = ex / jnp.sum(ex, axis=0, keepdims=True)
    p1 = jnp.max(probs, axis=0, keepdims=True)
    i1 = jnp.min(jnp.where(probs == p1, exp_id, n_exp), axis=0, keepdims=True)
    probs2 = jnp.where(exp_id == i1, -1.0, probs)
    p2 = jnp.max(probs2, axis=0, keepdims=True)
    i2 = jnp.min(jnp.where(probs2 == p2, exp_id, n_exp), axis=0, keepdims=True)
    den = p1 + p2

    onehot = ((exp_id == i1) | (exp_id == i2)).astype(F32)
    r_i = lax.broadcasted_iota(jnp.int32, (tm, tm), 0)
    c_i = lax.broadcasted_iota(jnp.int32, (tm, tm), 1)
    earlier = (r_i < c_i).astype(BF16)
    ranks = _dot(onehot.astype(BF16), earlier) + carry_s[:, 0:1]
    r1 = jnp.sum(jnp.where(exp_id == i1, ranks, 0.0), axis=0, keepdims=True)
    r2 = jnp.sum(jnp.where(exp_id == i2, ranks, 0.0), axis=0, keepdims=True)
    carry_s[...] = carry_s[...] + jnp.sum(onehot, axis=1, keepdims=True)
    cnt_ref[...] = carry_s[...]

    fields = (i1.astype(F32), i2.astype(F32), p1 / den, p2 / den, r1, r2)
    route = jnp.zeros((n_exp, tm), F32)
    for k, v in enumerate(fields):
        route = jnp.where(exp_id == k, v, route)
    route_ref[...] = route


def _oproj_router(attn, h, w_o, g, w_router, b_router):
    m, d = h.shape
    qk = attn.shape[1]
    n_exp = w_router.shape[1]
    assert n_exp == SUBLANE
    tm = _tile(m, 512, LANE)
    br = jnp.broadcast_to(b_router[:, None], (n_exp, LANE))
    kern = functools.partial(_oproj_router_kernel, n_exp=n_exp)
    row = lambda w: pl.BlockSpec((tm, w), lambda i: (i, 0))
    return pl.pallas_call(
        kern,
        grid=(m // tm,),
        in_specs=[row(qk), row(d), _full((qk, d)), _full((1, d)), _full((n_exp, d)), _full((n_exp, LANE))],
        out_specs=[row(d), row(d), pl.BlockSpec((n_exp, tm), lambda i: (0, i)), _full((n_exp, LANE))],
        out_shape=[jax.ShapeDtypeStruct((m, d), F32), jax.ShapeDtypeStruct((m, d), F32),
                   jax.ShapeDtypeStruct((n_exp, m), F32), jax.ShapeDtypeStruct((n_exp, LANE), F32)],
        scratch_shapes=[pltpu.VMEM((n_exp, LANE), F32)],
        compiler_params=_cparams("arbitrary"),
        name="oproj_router",
    )(attn, h, w_o.astype(BF16), g.reshape(1, d), w_router.T, br)


ISSUE_UNROLL = 8


def _row_copy_wait(src, dst, sem, rows):
    pltpu.make_async_copy(src.at[pl.ds(0, rows)], dst.at[pl.ds(0, rows)], sem).wait()


def _dispatch_kernel(pos_ref, hn_ref, xs_in_ref, xs_ref, sem, *, tm):
    del xs_in_ref
    i = pl.program_id(0)
    m = tm * pl.num_programs(0)

    def body(r, c):
        for k in range(TOP_K):
            dst = pos_ref[k * m + i * tm + r]
            pltpu.make_async_copy(hn_ref.at[pl.ds(r, 1)], xs_ref.at[pl.ds(dst, 1)], sem).start()
        return c

    lax.fori_loop(0, tm, body, 0, unroll=ISSUE_UNROLL)
    for _ in range(TOP_K):
        _row_copy_wait(hn_ref, xs_ref, sem, tm)


def _dispatch(hn, pos, n_sorted):
    m, d = hn.shape
    tm = _tile(m, 512)
    kern = functools.partial(_dispatch_kernel, tm=tm)
    return pl.pallas_call(
        kern,
        grid_spec=pltpu.PrefetchScalarGridSpec(
            num_scalar_prefetch=1,
            grid=(m // tm,),
            in_specs=[pl.BlockSpec((tm, d), lambda i, pos_ref: (i, 0)),
                      pl.BlockSpec(memory_space=pl.ANY)],
            out_specs=pl.BlockSpec(memory_space=pl.ANY),
            scratch_shapes=[pltpu.SemaphoreType.DMA(())]),
        out_shape=jax.ShapeDtypeStruct((n_sorted, d), F32),
        input_output_aliases={2: 0},
        compiler_params=_cparams("arbitrary"),
        name="moe_dispatch",
    )(pos, hn, jnp.zeros((n_sorted, d), F32))


def _experts_kernel(te_ref, tr_ref, x_ref, wg_ref, wu_ref, wd_ref, y_ref, xb_s, acc_s, *, row_step):
    del te_ref
    i = pl.program_id(0)
    j = pl.program_id(1)
    n_rows = tr_ref[i]
    valid = n_rows > 0

    @pl.when(valid & (j == 0))
    def _():
        xb_s[...] = x_ref[...].astype(BF16)
        acc_s[...] = jnp.zeros(acc_s.shape, F32)

    def swiglu_rows(rows):
        x = xb_s[0:rows, :]
        gate = _dot(x, wg_ref[0].astype(BF16))
        up = _dot(x, wu_ref[0].astype(BF16))
        act = (jax.nn.silu(gate) * up).astype(BF16)
        acc_s[0:rows, :] += _dot(act, wd_ref[0].astype(BF16))

    tm = xb_s.shape[0]
    for rows in range(row_step, tm + 1, row_step):
        pl.when((n_rows > rows - row_step) & (n_rows <= rows))(functools.partial(swiglu_rows, rows))

    last = j == pl.num_programs(1) - 1

    @pl.when(valid & last)
    def _():
        y_ref[...] = acc_s[...]

    @pl.when(jnp.logical_not(valid) & last)
    def _():
        y_ref[...] = jnp.zeros(y_ref.shape, F32)


def _experts(xs, tile_expert, tile_rows, w_gu, w_down, tm):
    ns, d = xs.shape
    f = w_down.shape[1]
    tf = _tile(f, 512, LANE)
    nf = f // tf
    nt = ns // tm
    row_step = tm // 4 if tm % (4 * MXU_WIDTH) == 0 else tm

    def jj(i, j, tv):
        return jnp.where(tv[i] > 0, j, nf - 1)

    return pl.pallas_call(
        functools.partial(_experts_kernel, row_step=row_step),
        grid_spec=pltpu.PrefetchScalarGridSpec(
            num_scalar_prefetch=2,
            grid=(nt, nf),
            in_specs=[pl.BlockSpec((tm, d), lambda i, j, te, tv: (i, 0)),
                      pl.BlockSpec((1, d, tf), lambda i, j, te, tv: (te[i], 0, jj(i, j, tv))),
                      pl.BlockSpec((1, d, tf), lambda i, j, te, tv: (te[i], 0, nf + jj(i, j, tv))),
                      pl.BlockSpec((1, tf, d), lambda i, j, te, tv: (te[i], jj(i, j, tv), 0))],
            out_specs=pl.BlockSpec((tm, d), lambda i, j, te, tv: (i, 0)),
            scratch_shapes=[pltpu.VMEM((tm, d), BF16), pltpu.VMEM((tm, d), F32)]),
        out_shape=jax.ShapeDtypeStruct((ns, d), F32),
        compiler_params=_cparams("arbitrary", "arbitrary"),
        name="moe_experts",
    )(tile_expert, tile_rows, xs, w_gu, w_gu, w_down)


def _combine_kernel(pos_ref, ys_ref, h_ref, w_ref, p_ref, gp_ref, wpg_ref, wpp_ref, gf_ref, o_ref,
                    buf_s, sem, *, tm):
    i = pl.program_id(0)
    n = pl.num_programs(0)
    m = tm * n

    def issue(step, slot):
        def body(r, c):
            for k in range(TOP_K):
                src = pos_ref[k * m + step * tm + r]
                pltpu.make_async_copy(ys_ref.at[pl.ds(src, 1)], buf_s.at[slot, k, pl.ds(r, 1)],
                                      sem.at[slot]).start()
            return c
        lax.fori_loop(0, tm, body, 0, unroll=ISSUE_UNROLL)

    @pl.when(i == 0)
    def _():
        issue(0, 0)

    slot = lax.rem(i, 2)

    @pl.when(i + 1 < n)
    def _():
        issue(i + 1, 1 - slot)

    for k in range(TOP_K):
        _row_copy_wait(ys_ref, buf_s.at[slot, k], sem.at[slot], tm)

    w = w_ref[...]
    h = h_ref[...] + w[:, 0:1] * buf_s[slot, 0] + w[:, 1:2] * buf_s[slot, 1]
    h = _ple(h, p_ref, gp_ref, wpg_ref, wpp_ref)
    o_ref[...] = _rms(h, gf_ref[...])


def _combine(ys, pos, h, weights, p_layers, layer, g_ple, w_pgate, w_pproj, g_final):
    m, d = h.shape
    pd = p_layers.shape[2]
    tm = _tile(m, 256)
    kern = functools.partial(_combine_kernel, tm=tm)
    row = lambda w: pl.BlockSpec((tm, w), lambda i, pos_ref: (i, 0))
    const = lambda s: pl.BlockSpec(s, lambda i, pos_ref: (0, 0))
    return pl.pallas_call(
        kern,
        grid_spec=pltpu.PrefetchScalarGridSpec(
            num_scalar_prefetch=1,
            grid=(m // tm,),
            in_specs=[pl.BlockSpec(memory_space=pl.ANY), row(d), row(TOP_K),
                      pl.BlockSpec((None, tm, pd), lambda i, pos_ref: (layer, i, 0)), const((1, d)),
                      const((d, d)), const((pd, d)), const((1, d))],
            out_specs=row(d),
            scratch_shapes=[pltpu.VMEM((2, TOP_K, tm, d), F32), pltpu.SemaphoreType.DMA((2,))]),
        out_shape=jax.ShapeDtypeStruct((m, d), F32),
        compiler_params=_cparams("arbitrary"),
        name="moe_combine",
    )(pos, ys, h, weights, p_layers, g_ple.reshape(1, d), w_pgate, w_pproj, g_final.reshape(1, d))


def _route_plan(route, counts, n_exp, tm, n_tiles):
    e = route[0:TOP_K].astype(jnp.int32)
    rank = route[4:4 + TOP_K].astype(jnp.int32)
    cnt = counts[:, 0].astype(jnp.int32)
    ntile = (cnt + tm - 1) // tm
    tile_end = jnp.cumsum(ntile)
    row_start = (tile_end - ntile) * tm
    sel = e[:, :, None] == jnp.arange(n_exp)[None, None, :]
    pos = jnp.sum(jnp.where(sel, row_start[None, None, :], 0), axis=-1) + rank
    tiles = jnp.arange(n_tiles)
    total = tile_end[-1]
    te = jnp.minimum(jnp.sum(tiles[:, None] >= tile_end[None, :], axis=1), n_exp - 1)
    valid = tiles < total
    te_last = jnp.sum(jnp.where(tiles == total - 1, te, 0))
    te = jnp.where(valid, te, te_last).astype(jnp.int32)
    own = te[:, None] == jnp.arange(n_exp)[None, :]
    tile_in_group = tiles - jnp.sum(jnp.where(own, (tile_end - ntile)[None, :], 0), axis=1)
    group_rows = jnp.sum(jnp.where(own, cnt[None, :], 0), axis=1)
    tile_rows = jnp.where(valid, jnp.clip(group_rows - tile_in_group * tm, 0, tm), 0)
    return pos.reshape(-1).astype(jnp.int32), te, tile_rows.astype(jnp.int32)


def _moe_and_final(attn, h, w_o, g_ffn, w_router, b_router, w_gu, w_down, p_layers, layer, g_ple, w_pgate,
                   w_pproj, g_final, *, tm_expert):
    m, d = h.shape
    n_exp = w_router.shape[1]
    h4, hn, route, counts = _oproj_router(attn, h, w_o, g_ffn, w_router, b_router)
    n_tiles = -(-(m * TOP_K) // tm_expert) + n_exp
    pos, te, tv = _route_plan(route, counts, n_exp, tm_expert, n_tiles)
    xs = _dispatch(hn, pos, n_tiles * tm_expert)
    ys = _experts(xs, te, tv, w_gu, w_down, tm_expert)
    weights = route[2:2 + TOP_K].T
    return _combine(ys, pos, h4, weights, p_layers, layer, g_ple, w_pgate, w_pproj, g_final)


def kernel(x_prompt, x_sample, state_h, state_conv, cache_k, cache_v, cache_logf, page_table, p_prompt, p_sample, g_mix_a, w_in_a, conv_w_a, conv_b_a, w_rg_a, b_rg_a, w_ig_a, b_ig_a, lru_lambda_a, w_out_a, g_kv, w_kv, b_f, g_mix_b, w_q_b, w_o_b, g_ffn, w_gu_dense, w_down_dense, w_router, b_router, w_gu_moe, w_down_moe, g_ple, w_ple_gate, w_ple_proj, g_final):
    assert w_in_a.shape[0] == 1 and w_q_b.shape[0] == 1 and g_ffn.shape[0] == 2
    assert x_sample.shape[1] == 1
    bsz, t_len, d = x_prompt.shape
    dec_b = x_sample.shape[0]
    nh = b_f.shape[0]
    qk = w_q_b.shape[2]
    hd = qk // nh
    r = w_out_a.shape[1]
    scale = hd ** -0.5
    mp = bsz * t_len

    wins, rec_w = _rec_weights(w_in_a[0], conv_w_a[0], conv_b_a[0], w_rg_a[0], b_rg_a[0], w_ig_a[0],
                               b_ig_a[0], lru_lambda_a[0], w_out_a[0])
    w_gu_d = w_gu_dense[0].astype(BF16)
    w_down_d = w_down_dense[0].astype(BF16)
    w_pg = [w_ple_gate[l].astype(BF16) for l in range(2)]
    w_pp = [w_ple_proj[l].astype(BF16) for l in range(2)]

    def rest_of_step(h1, p, seq_len):
        h3 = _ffn_dense(h1, g_ffn[0], w_gu_d, w_down_d, p, 0, g_ple[0], w_pg[0], w_pp[0])
        return h3, _kvq(h3, g_kv, g_mix_b[0], w_kv, b_f, w_q_b[0], seq_len=seq_len)

    def moe(attn, h3, p, tm_expert):
        return _moe_and_final(attn, h3, w_o_b[0], g_ffn[1], w_router[0], b_router[0], w_gu_moe[0],
                              w_down_moe[0], p, 1, g_ple[1], w_pg[1], w_pp[1], g_final, tm_expert=tm_expert)

    h1, hl_p, cn_p = _rec_prompt(x_prompt, g_mix_a[0], wins, rec_w)
    pp = p_prompt.reshape(2, mp, -1)
    h3, (k_p, v_p, q_p, lft_p) = rest_of_step(h1.reshape(mp, d), pp, t_len)
    attn_p = _attn_prompt(q_p.reshape(bsz, t_len, qk), k_p.reshape(bsz, t_len, qk), v_p.reshape(bsz, t_len, qk),
                          lft_p, nh=nh, scale=scale)
    lf_p = jnp.swapaxes(lft_p, 1, 2)
    y_p = moe(attn_p.reshape(mp, qk), h3, pp, min(1024, mp))

    h1s, hl_s, cn_s = _rec_sample(x_sample.reshape(dec_b, d), g_mix_a[0], wins, rec_w, state_conv[0], state_h[0])
    ps = p_sample.reshape(2, dec_b, -1)
    h3s, (k_s, v_s, q_s, lf_s) = rest_of_step(h1s, ps, None)
    attn_s = _attn_sample(q_s.astype(F32), k_s, v_s, lf_s, cache_k, cache_v, cache_logf, page_table, scale=scale)
    y_s = moe(attn_s, h3s, ps, dec_b)

    return (y_p.reshape(bsz, t_len, d), y_s.reshape(dec_b, 1, d),
            hl_p[None], cn_p[None],
            k_p.reshape(bsz, t_len, nh, hd), v_p.reshape(bsz, t_len, nh, hd), lf_p,
            hl_s[None], cn_s[None],
            k_s.reshape(dec_b, 1, nh, hd), v_s.reshape(dec_b, 1, nh, hd), lf_s.reshape(dec_b, 1, nh))
```

```python
import functools

import jax
import jax.numpy as jnp
from jax import lax
from jax.experimental import pallas as pl
from jax.experimental.pallas import tpu as pltpu

F32 = jnp.float32
BF16 = jnp.bfloat16

EPS = 1e-6
LRU_C = 8.0
N_LRU_BLOCKS = 8
CONV_W = 4
N_EXPERTS = 8
TOP_K = 2

LANE = 128
SUBLANE = 8
MXU_WIDTH = 256
VMEM_BYTES = 64 * 1024 * 1024
VMEM_LIMIT = VMEM_BYTES * 7 // 8
NEG_BIG = -1e30
LOG2E = 1.4426950408889634


def _cparams(*sem):
    return pltpu.CompilerParams(dimension_semantics=sem, vmem_limit_bytes=VMEM_LIMIT)


def _tile(n, pref, align=SUBLANE):
    if n <= pref:
        return n
    t = pref // align * align
    while n % t:
        t -= align
    return t


def _full(shape):
    nd = len(shape)
    return pl.BlockSpec(shape, lambda *_: (0,) * nd)


def _rms(x, g):
    ms = jnp.mean(x * x, axis=-1, keepdims=True)
    return x * lax.rsqrt(ms + EPS) * g


def _dot(a, b):
    return jnp.dot(a, b, preferred_element_type=F32)


def _split3_f32(x):
    hi = x.astype(BF16).astype(F32)
    r1 = x - hi
    mid = r1.astype(BF16).astype(F32)
    lo = (r1 - mid).astype(BF16).astype(F32)
    return hi, mid, lo


def _split3(x):
    return tuple(piece.astype(BF16) for piece in _split3_f32(x))


def _band_windows(r, nblk, tw):
    blk = r // nblk
    wins = []
    c0 = 0
    while c0 < r:
        c1 = min(c0 + tw, r)
        b0, b1 = c0 // blk, (c1 - 1) // blk
        k0 = (b0 * blk) // LANE * LANE
        k1 = min(r, -(-((b1 + 1) * blk) // LANE) * LANE)
        wins.append((c0, c1, k0, k1))
        c0 = c1
    return tuple(wins)


def _band_pack(ws, wins, tw):
    parts = []
    dense = [jax.scipy.linalg.block_diag(*w) for w in ws]
    for c0, c1, k0, k1 in wins:
        parts.append(jnp.concatenate(
            [jnp.pad(dn[k0:k1, c0:c1], ((0, 0), (0, tw - (c1 - c0)))) for dn in dense], axis=1))
    return jnp.concatenate(parts, axis=0).astype(BF16)


def _band_matmul(x_bf, w_ref, wins):
    tw = w_ref.shape[1] // 2
    results = []
    for which in range(2):
        outs = []
        off = 0
        for c0, c1, k0, k1 in wins:
            o = _dot(x_bf[:, k0:k1], w_ref[off:off + (k1 - k0), which * tw:(which + 1) * tw])
            outs.append(o[:, :c1 - c0])
            off += k1 - k0
        results.append(jnp.concatenate(outs, axis=1) if len(outs) > 1 else outs[0])
    return tuple(results)


def _lru_gates(uc, rg, ig, lam):
    r = jax.nn.sigmoid(rg)
    i = jax.nn.sigmoid(ig)
    log_a = (-LRU_C) * r * jax.nn.softplus(-lam)
    a = jnp.exp(log_a)
    one_minus_a2 = jnp.tanh(-log_a) * (a * a + 1.0)
    return a, jnp.sqrt(one_minus_a2) * (i * uc)


def _rec_prompt_kernel(x_ref, g_ref, win_ref, cw_ref, cb_ref, wband_ref, brg_ref, big_ref,
                       lam_ref, wout_ref, h_ref, hl_ref, cn_ref, tail_s, a_s, xi_s, hs_s, hc_s,
                       *, wins, tc, r):
    t = pl.program_id(1)

    @pl.when(t == 0)
    def _():
        tail_s[...] = jnp.zeros(tail_s.shape, F32)
        hc_s[...] = jnp.zeros(hc_s.shape, F32)

    x = x_ref[0]
    xn = _rms(x, g_ref[...]).astype(BF16)
    gu = _dot(xn, win_ref[...])
    gate = gu[:, :r]
    u = gu[:, r:]

    tail = tail_s[...]
    row8 = lax.broadcasted_iota(jnp.int32, (SUBLANE, r), 0)
    uc = cb_ref[...]
    for j in range(CONV_W):
        k = CONV_W - 1 - j
        if k:
            sh = pltpu.roll(u, k, 0)
            head = jnp.where(row8 < k, pltpu.roll(tail, k, 0), sh[0:SUBLANE])
            sh = jnp.concatenate([head, sh[SUBLANE:]], axis=0)
        else:
            sh = u
        uc = uc + sh * cw_ref[j:j + 1, :]
    tail_s[...] = u[tc - SUBLANE:tc]

    @pl.when(t == pl.num_programs(1) - 1)
    def _():
        cn_ref[0] = u[tc - (CONV_W - 1):tc]

    rg, ig = _band_matmul(uc.astype(BF16), wband_ref, wins)
    a, xin = _lru_gates(uc, rg + brg_ref[...], ig + big_ref[...], lam_ref[...])
    a_s[...] = a
    xi_s[...] = xin

    row = lax.broadcasted_iota(jnp.int32, (SUBLANE, r), 0)

    def body(gi, h_in):
        off = pl.multiple_of(gi * SUBLANE, SUBLANE)
        a8 = a_s[pl.ds(off, SUBLANE), :]
        x8 = xi_s[pl.ds(off, SUBLANE), :]
        for k in (1, 2, 4):
            m = row >= k
            x8 = x8 + a8 * jnp.where(m, pltpu.roll(x8, k, 0), 0.0)
            a8 = a8 * jnp.where(m, pltpu.roll(a8, k, 0), 1.0)
        h8 = a8 * h_in + x8
        hs_s[pl.ds(off, SUBLANE), :] = h8
        return h8[SUBLANE - 1:SUBLANE, :]

    h_last = lax.fori_loop(0, tc // SUBLANE, body, hc_s[0:1, :])
    hc_s[0:1, :] = h_last
    hl_ref[0] = h_last

    y = (hs_s[...] * jax.nn.gelu(gate)).astype(BF16)
    h_ref[0] = x + _dot(y, wout_ref[...])


def _rec_sample_kernel(x_ref, g_ref, win_ref, cw_ref, cb_ref, wband_ref, brg_ref, big_ref,
                       lam_ref, wout_ref, c0_ref, c1_ref, c2_ref, h0_ref, h_ref, hl_ref, u_ref,
                       *, wins, r):
    x = x_ref[...]
    xn = _rms(x, g_ref[...]).astype(BF16)
    gu = _dot(xn, win_ref[...])
    gate = gu[:, :r]
    u = gu[:, r:]
    uc = cb_ref[...] + c0_ref[...] * cw_ref[0:1, :]
    uc = uc + c1_ref[...] * cw_ref[1:2, :]
    uc = uc + c2_ref[...] * cw_ref[2:3, :]
    uc = uc + u * cw_ref[3:4, :]
    rg, ig = _band_matmul(uc.astype(BF16), wband_ref, wins)
    a, xin = _lru_gates(uc, rg + brg_ref[...], ig + big_ref[...], lam_ref[...])
    h = a * h0_ref[...] + xin
    hl_ref[...] = h
    u_ref[...] = u
    y = (h * jax.nn.gelu(gate)).astype(BF16)
    h_ref[...] = x + _dot(y, wout_ref[...])


def _rec_weights(w_in, conv_w, conv_b, w_rg, b_rg, w_ig, b_ig, lam, w_out):
    r = w_out.shape[0]
    tw = min(MXU_WIDTH, r)
    wins = _band_windows(r, w_rg.shape[0], tw)
    args = (w_in.astype(BF16), conv_w, conv_b.reshape(1, r), _band_pack((w_rg, w_ig), wins, tw),
            b_rg.reshape(1, r), b_ig.reshape(1, r), lam.reshape(1, r), w_out.astype(BF16))
    return wins, args


def _rec_prompt(x, g, wins, wargs):
    b, t, d = x.shape
    r = wargs[-1].shape[0]
    tc = _tile(t, 256)
    kern = functools.partial(_rec_prompt_kernel, wins=wins, tc=tc, r=r)
    wspecs = [_full(a.shape) for a in wargs]
    h, hl, cn = pl.pallas_call(
        kern,
        grid=(b, t // tc),
        in_specs=[pl.BlockSpec((1, tc, d), lambda i, j: (i, j, 0)), _full((1, d))] + wspecs,
        out_specs=[pl.BlockSpec((1, tc, d), lambda i, j: (i, j, 0)),
                   pl.BlockSpec((1, 1, r), lambda i, j: (i, 0, 0)),
                   pl.BlockSpec((1, CONV_W - 1, r), lambda i, j: (i, 0, 0))],
        out_shape=[jax.ShapeDtypeStruct((b, t, d), F32),
                   jax.ShapeDtypeStruct((b, 1, r), F32),
                   jax.ShapeDtypeStruct((b, CONV_W - 1, r), F32)],
        scratch_shapes=[pltpu.VMEM((SUBLANE, r), F32), pltpu.VMEM((tc, r), F32),
                        pltpu.VMEM((tc, r), F32), pltpu.VMEM((tc, r), F32), pltpu.VMEM((SUBLANE, r), F32)],
        compiler_params=_cparams("arbitrary", "arbitrary"),
        name="rec_prompt",
    )(x, g.reshape(1, d), *wargs)
    return h, hl.reshape(b, r), cn


def _rec_sample(x, g, wins, wargs, conv0, h0):
    m, d = x.shape
    r = wargs[-1].shape[0]
    kern = functools.partial(_rec_sample_kernel, wins=wins, r=r)
    h, hl, u = pl.pallas_call(
        kern,
        out_shape=[jax.ShapeDtypeStruct((m, d), F32), jax.ShapeDtypeStruct((m, r), F32),
                   jax.ShapeDtypeStruct((m, r), F32)],
        compiler_params=pltpu.CompilerParams(vmem_limit_bytes=VMEM_LIMIT),
        name="rec_sample",
    )(x, g.reshape(1, d), *wargs, conv0[:, 0], conv0[:, 1], conv0[:, 2], h0)
    conv_new = jnp.stack([conv0[:, 1], conv0[:, 2], u], axis=1)
    return h, hl, conv_new


def _ple(h, p_ref, g_ref, wgate_ref, wproj_ref):
    gate = jax.nn.sigmoid(_dot(_rms(h, g_ref[...]).astype(BF16), wgate_ref[...]))
    return h + gate * _dot(p_ref[...].astype(BF16), wproj_ref[...])


def _ffn_dense_kernel(h_ref, g_ref, wg_ref, wu_ref, wd_ref, p_ref, gp_ref, wpg_ref, wpp_ref,
                      o_ref, hn_s, acc_s):
    j = pl.program_id(1)

    @pl.when(j == 0)
    def _():
        hn_s[...] = _rms(h_ref[...], g_ref[...]).astype(BF16)
        acc_s[...] = jnp.zeros(acc_s.shape, F32)

    hn = hn_s[...]
    act = (jax.nn.silu(_dot(hn, wg_ref[...])) * _dot(hn, wu_ref[...])).astype(BF16)
    acc_s[...] += _dot(act, wd_ref[...])

    @pl.when(j == pl.num_programs(1) - 1)
    def _():
        o_ref[...] = _ple(h_ref[...] + acc_s[...], p_ref, gp_ref, wpg_ref, wpp_ref)


def _ffn_dense(h, g, w_gu, w_down, p_layers, layer, g_ple, w_pgate, w_pproj):
    m, d = h.shape
    f = w_down.shape[0]
    pd = p_layers.shape[2]
    tm = _tile(m, 512)
    tf = _tile(f, 512, LANE)
    nf = f // tf
    return pl.pallas_call(
        _ffn_dense_kernel,
        grid=(m // tm, nf),
        in_specs=[pl.BlockSpec((tm, d), lambda i, j: (i, 0)), _full((1, d)),
                  pl.BlockSpec((d, tf), lambda i, j: (0, j)),
                  pl.BlockSpec((d, tf), lambda i, j: (0, nf + j)),
                  pl.BlockSpec((tf, d), lambda i, j: (j, 0)),
                  pl.BlockSpec((None, tm, pd), lambda i, j: (layer, i, 0)), _full((1, d)),
                  _full((d, d)), _full((pd, d))],
        out_specs=pl.BlockSpec((tm, d), lambda i, j: (i, 0)),
        out_shape=jax.ShapeDtypeStruct((m, d), F32),
        scratch_shapes=[pltpu.VMEM((tm, d), BF16), pltpu.VMEM((tm, d), F32)],
        compiler_params=_cparams("arbitrary", "arbitrary"),
        name="ffn_dense",
    )(h, g.reshape(1, d), w_gu, w_gu, w_down, p_layers, g_ple.reshape(1, d), w_pgate, w_pproj)


def _kvq_kernel(h_ref, gkv_ref, gq_ref, wkv_ref, bf_ref, wq_ref, k_ref, v_ref, q_ref, lf_ref,
                *, qk, nh, head_major):
    x = h_ref[...]
    xh = x * lax.rsqrt(jnp.mean(x * x, axis=-1, keepdims=True) + EPS)
    kvf = _dot((xh * gkv_ref[...]).astype(BF16), wkv_ref[...])
    k_ref[...] = kvf[:, :qk]
    v_ref[...] = kvf[:, qk:2 * qk]
    lf = jax.nn.log_sigmoid(kvf[:, 2 * qk:] + bf_ref[...])
    if head_major:
        lf_ref[0] = lf.T[:nh, :]
    else:
        lf_ref[...] = lf[:, :nh]
    q_ref[...] = _dot((xh * gq_ref[...]).astype(BF16), wq_ref[...]).astype(BF16)


def _kvq(h, g_kv, g_q, w_kv, b_f, w_q, *, seq_len=None):
    m, d = h.shape
    qk = w_q.shape[1]
    nh = b_f.shape[0]
    head_major = seq_len is not None
    tm = _tile(seq_len if head_major else m, 512, LANE if head_major else SUBLANE)
    wkv = jnp.pad(w_kv, ((0, 0), (0, LANE - nh))).astype(BF16)
    bfp = jnp.pad(b_f, (0, LANE - nh)).reshape(1, LANE)
    out_specs = [pl.BlockSpec((tm, qk), lambda i: (i, 0)), pl.BlockSpec((tm, qk), lambda i: (i, 0)),
                 pl.BlockSpec((tm, qk), lambda i: (i, 0))]
    out_shape = [jax.ShapeDtypeStruct((m, qk), F32), jax.ShapeDtypeStruct((m, qk), F32),
                 jax.ShapeDtypeStruct((m, qk), BF16)]
    if head_major:
        nt = seq_len // tm
        out_specs.append(pl.BlockSpec((1, nh, tm), lambda i: (i // nt, 0, i % nt)))
        out_shape.append(jax.ShapeDtypeStruct((m // seq_len, nh, seq_len), F32))
    else:
        out_specs.append(pl.BlockSpec((tm, nh), lambda i: (i, 0)))
        out_shape.append(jax.ShapeDtypeStruct((m, nh), F32))
    kern = functools.partial(_kvq_kernel, qk=qk, nh=nh, head_major=head_major)
    return pl.pallas_call(
        kern,
        grid=(m // tm,),
        in_specs=[pl.BlockSpec((tm, d), lambda i: (i, 0)), _full((1, d)), _full((1, d)),
                  _full(wkv.shape), _full((1, LANE)), _full((d, qk))],
        out_specs=out_specs,
        out_shape=out_shape,
        compiler_params=_cparams("arbitrary"),
        name="kvq",
    )(h, g_kv.reshape(1, d), g_q.reshape(1, d), wkv, bfp, w_q.astype(BF16))


def _cumsum_lanes(x):
    n = x.shape[1]
    lane = lax.broadcasted_iota(jnp.int32, x.shape, 1)
    k = 1
    while k < n:
        x = x + jnp.where(lane >= k, pltpu.roll(x, k, 1), 0.0)
        k *= 2
    return x


def _fold_lanes(x, op):
    r = x[:, :LANE]
    for c in range(1, x.shape[1] // LANE):
        r = op(r, x[:, c * LANE:(c + 1) * LANE])
    return r


def _attn_prompt_kernel(q_ref, k_ref, v_ref, lft_ref, o_ref, kb_s, vb_s, c2_s, s_s, p_s, *, tq, scale):
    h = pl.program_id(1)
    nblk = c2_s.shape[0]
    kb_s[...] = k_ref[0].astype(BF16)
    vb_s[...] = v_ref[0].astype(BF16)

    @pl.when(h == 0)
    def _():
        c = _cumsum_lanes(lft_ref[0]) * LOG2E
        for j in range(nblk):
            c2_s[j] = c[:, j * tq:(j + 1) * tq]

    c1 = scale * LOG2E
    row = lax.broadcasted_iota(jnp.int32, (tq, tq), 0)
    col = lax.broadcasted_iota(jnp.int32, (tq, tq), 1)
    causal = col <= row

    def scores(n):
        q = q_ref[0, n * tq:(n + 1) * tq, :]
        base = n * (n + 1) // 2
        mrun = jnp.full((tq, LANE), NEG_BIG, F32)
        for kb in range(n + 1):
            s = lax.dot_general(q, kb_s[kb * tq:(kb + 1) * tq, :], (((1,), (1,)), ((), ())),
                                preferred_element_type=F32) * c1
            s = s - c2_s[kb, pl.ds(h, 1), :]
            if kb == n:
                s = jnp.where(causal, s, NEG_BIG)
            s_s[base + kb] = s
            mrun = jnp.maximum(mrun, _fold_lanes(s, jnp.maximum))
        return jnp.max(mrun, axis=1, keepdims=True)

    m = scores(0)
    for n in range(nblk):
        m_next = scores(n + 1) if n + 1 < nblk else None
        base = n * (n + 1) // 2
        lrun = jnp.zeros((tq, LANE), F32)
        for kb in range(n + 1):
            p = jnp.exp2(s_s[base + kb] - m)
            lrun = lrun + _fold_lanes(p, jnp.add)
            p_s[n, :, kb * tq:(kb + 1) * tq] = p.astype(BF16)
        acc = _dot(p_s[n, :, :(n + 1) * tq], vb_s[:(n + 1) * tq, :])
        o_ref[0, n * tq:(n + 1) * tq, :] = (acc / jnp.sum(lrun, axis=1, keepdims=True)).astype(BF16)
        m = m_next


def _attn_prompt(q, k, v, lft, *, nh, scale):
    b, t, qk = q.shape
    hd = qk // nh
    tq = _tile(t, 256, LANE)
    nq = t // tq
    kern = functools.partial(_attn_prompt_kernel, tq=tq, scale=scale)
    head = pl.BlockSpec((1, t, hd), lambda i, h: (i, 0, h))
    return pl.pallas_call(
        kern,
        grid=(b, nh),
        in_specs=[head, head, head, pl.BlockSpec((1, nh, t), lambda i, h: (i, 0, 0))],
        out_specs=head,
        out_shape=jax.ShapeDtypeStruct((b, t, qk), BF16),
        scratch_shapes=[pltpu.VMEM((t, hd), BF16), pltpu.VMEM((t, hd), BF16),
                        pltpu.VMEM((nq, nh, tq), F32), pltpu.VMEM((nq * (nq + 1) // 2, tq, tq), F32),
                        pltpu.VMEM((nq, tq, t), BF16)],
        compiler_params=_cparams("arbitrary", "arbitrary"),
        name="attn_prompt",
    )(q, k, v, lft)


def _attn_sample_kernel(pt_ref, q_ref, kn_ref, vn_ref, lfn_ref, later_ref, *rest, pp, nh, scale):
    k_refs = rest[0:pp]
    v_refs = rest[pp:2 * pp]
    lf_refs = rest[2 * pp:3 * pp]
    o_ref = rest[3 * pp]
    m_s, l_s, acc_s, carry_s = rest[3 * pp + 1:]
    del pt_ref
    p = pl.program_id(1)
    q = q_ref[0].astype(BF16)

    @pl.when(p == 0)
    def _():
        kn = kn_ref[0].astype(BF16).astype(F32)
        s_new = jnp.sum(q.astype(F32) * kn, axis=1, keepdims=True) * scale
        m_s[...] = jnp.broadcast_to(s_new, m_s.shape)
        l_s[...] = jnp.ones(l_s.shape, F32)
        acc_s[...] = vn_ref[0]
        carry_s[...] = lfn_ref[0]

    scores = [lax.dot_general(q, k_refs[j][0].astype(BF16), (((1,), (1,)), ((), ())),
                              preferred_element_type=F32) for j in range(pp)]

    lfs = [lf_refs[j][0] for j in range(pp)]
    pieces = []
    for lf in lfs:
        pieces.extend(_split3_f32(lf))
    later = _dot(jnp.concatenate(pieces, axis=0).astype(BF16), later_ref[...])
    rows_per_page = later.shape[1]
    row = lax.broadcasted_iota(jnp.int32, (nh, rows_per_page), 0)
    lane = lax.broadcasted_iota(jnp.int32, (nh, rows_per_page), 1)
    own_head = jnp.bitwise_and(lane, nh - 1) == row

    carry = carry_s[...]
    probs = []
    for j in range(pp):
        bias = (later[3 * j * nh:(3 * j + 1) * nh] + later[(3 * j + 1) * nh:(3 * j + 2) * nh]
                + later[(3 * j + 2) * nh:(3 * j + 3) * nh]) + carry[:, 0:1]
        carry = carry + jnp.sum(lfs[j], axis=1, keepdims=True)
        s = jnp.where(own_head, scores[j] * scale + bias, NEG_BIG)
        m_j = jnp.max(s, axis=1, keepdims=True)
        pr = jnp.exp(s - m_j)
        probs.append((m_j, jnp.sum(pr, axis=1, keepdims=True), pr.astype(BF16)))
    carry_s[...] = carry
    parts = [(m_j, l_j, _dot(pr, v_refs[j][0].astype(BF16))) for j, (m_j, l_j, pr) in enumerate(probs)]

    m_old = m_s[...]
    m_new = m_old
    for m_j, _, _ in parts:
        m_new = jnp.maximum(m_new, m_j)
    alpha = jnp.exp(m_old - m_new)
    l_new = alpha * l_s[...]
    acc = alpha * acc_s[...]
    for m_j, l_j, o_j in parts:
        w_j = jnp.exp(m_j - m_new)
        l_new = l_new + w_j * l_j
        acc = acc + w_j * o_j
    m_s[...] = m_new
    l_s[...] = l_new
    acc_s[...] = acc

    @pl.when(p == pl.num_programs(1) - 1)
    def _():
        o_ref[0] = acc_s[...] / l_s[...]


def _attn_sample(q, k_new, v_new, lf_new, cache_k, cache_v, cache_logf, page_table, *, scale):
    b, qk = q.shape
    n_phys, ps, nh, hd = cache_k.shape
    assert ps == LANE and hd == LANE and nh == SUBLANE
    n_pages = page_table.shape[1]
    pp = 8
    while n_pages % pp:
        pp //= 2
    ck = cache_k.reshape(n_phys, ps * nh, hd)
    cv = cache_v.reshape(n_phys, ps * nh, hd)
    clf = jnp.swapaxes(cache_logf, 1, 2)
    pt = page_table.reshape(-1)
    lfn = jnp.broadcast_to(lf_new[:, :, None], (b, nh, LANE))
    later = (jnp.arange(ps)[:, None] > (jnp.arange(ps * nh) // nh)[None, :]).astype(BF16)

    def page_map(j):
        return lambda i, p, pt_ref: (pt_ref[i * n_pages + (n_pages - 1 - (p * pp + j))], 0, 0)

    head_spec = pl.BlockSpec((1, nh, hd), lambda i, p, pt_ref: (i, 0, 0))
    in_specs = [head_spec] * 4 + [pl.BlockSpec((ps, ps * nh), lambda i, p, pt_ref: (0, 0))]
    in_specs += [pl.BlockSpec((1, ps * nh, hd), page_map(j)) for j in range(pp)]
    in_specs += [pl.BlockSpec((1, ps * nh, hd), page_map(j)) for j in range(pp)]
    in_specs += [pl.BlockSpec((1, nh, ps), page_map(j)) for j in range(pp)]
    kern = functools.partial(_attn_sample_kernel, pp=pp, nh=nh, scale=scale)
    out = pl.pallas_call(
        kern,
        grid_spec=pltpu.PrefetchScalarGridSpec(
            num_scalar_prefetch=1,
            grid=(b, n_pages // pp),
            in_specs=in_specs,
            out_specs=head_spec,
            scratch_shapes=[pltpu.VMEM((nh, LANE), F32), pltpu.VMEM((nh, LANE), F32),
                            pltpu.VMEM((nh, hd), F32), pltpu.VMEM((nh, LANE), F32)]),
        out_shape=jax.ShapeDtypeStruct((b, nh, hd), F32),
        compiler_params=_cparams("arbitrary", "arbitrary"),
        name="attn_sample",
    )(pt, q.reshape(b, nh, hd), k_new.reshape(b, nh, hd), v_new.reshape(b, nh, hd), lfn, later,
      *([ck] * pp), *([cv] * pp), *([clf] * pp))
    return out.reshape(b, qk)


def _oproj_router_kernel(a_ref, h_ref, wo_ref, g_ref, wr_ref, br_ref, h4_ref, hn_ref, route_ref, cnt_ref,
                         carry_s, *, n_exp):
    i = pl.program_id(0)

    @pl.when(i == 0)
    def _():
        carry_s[...] = jnp.zeros(carry_s.shape, F32)

    h4 = h_ref[...] + _dot(a_ref[...].astype(BF16), wo_ref[...])
    h4_ref[...] = h4
    hn = _rms(h4, g_ref[...])
    hn_ref[...] = hn
    tm = hn.shape[0]

    def dot_nt(a, b):
        return lax.dot_general(a, b, (((1,), (1,)), ((), ())), preferred_element_type=F32)

    x_hi, x_mid, _ = _split3(hn)
    w_hi, w_mid, _ = _split3(wr_ref[...])
    logits = dot_nt(w_hi, x_hi) + dot_nt(w_hi, x_mid) + dot_nt(w_mid, x_hi) + br_ref[:, 0:1]

    exp_id = lax.broadcasted_iota(jnp.int32, (n_exp, tm), 0)
    ex = jnp.exp(logits - jnp.max(logits, axis=0, keepdims=True))
    probs = ex / jnp.sum(ex, axis=0, keepdims=True)
    p1 = jnp.max(probs, axis=0, keepdims=True)
    i1 = jnp.min(jnp.where(probs == p1, exp_id, n_exp), axis=0, keepdims=True)
    probs2 = jnp.where(exp_id == i1, -1.0, probs)
    p2 = jnp.max(probs2, axis=0, keepdims=True)
    i2 = jnp.min(jnp.where(probs2 == p2, exp_id, n_exp), axis=0, keepdims=True)
    den = p1 + p2

    onehot = ((exp_id == i1) | (exp_id == i2)).astype(F32)
    r_i = lax.broadcasted_iota(jnp.int32, (tm, tm), 0)
    c_i = lax.broadcasted_iota(jnp.int32, (tm, tm), 1)
    earlier = (r_i < c_i).astype(BF16)
    ranks = _dot(onehot.astype(BF16), earlier) + carry_s[:, 0:1]
    r1 = jnp.sum(jnp.where(exp_id == i1, ranks, 0.0), axis=0, keepdims=True)
    r2 = jnp.sum(jnp.where(exp_id == i2, ranks, 0.0), axis=0, keepdims=True)
    carry_s[...] = carry_s[...] + jnp.sum(onehot, axis=1, keepdims=True)
    cnt_ref[...] = carry_s[...]

    fields = (i1.astype(F32), i2.astype(F32), p1 / den, p2 / den, r1, r2)
    route = jnp.zeros((n_exp, tm), F32)
    for k, v in enumerate(fields):
        route = jnp.where(exp_id == k, v, route)
    route_ref[...] = route


def _oproj_router(attn, h, w_o, g, w_router, b_router):
    m, d = h.shape
    qk = attn.shape[1]
    n_exp = w_router.shape[1]
    assert n_exp == SUBLANE
    tm = _tile(m, 512, LANE)
    br = jnp.broadcast_to(b_router[:, None], (n_exp, LANE))
    kern = functools.partial(_oproj_router_kernel, n_exp=n_exp)
    row = lambda w: pl.BlockSpec((tm, w), lambda i: (i, 0))
    return pl.pallas_call(
        kern,
        grid=(m // tm,),
        in_specs=[row(qk), row(d), _full((qk, d)), _full((1, d)), _full((n_exp, d)), _full((n_exp, LANE))],
        out_specs=[row(d), row(d), pl.BlockSpec((n_exp, tm), lambda i: (0, i)), _full((n_exp, LANE))],
        out_shape=[jax.ShapeDtypeStruct((m, d), F32), jax.ShapeDtypeStruct((m, d), F32),
                   jax.ShapeDtypeStruct((n_exp, m), F32), jax.ShapeDtypeStruct((n_exp, LANE), F32)],
        scratch_shapes=[pltpu.VMEM((n_exp, LANE), F32)],
        compiler_params=_cparams("arbitrary"),
        name="oproj_router",
    )(attn, h, w_o.astype(BF16), g.reshape(1, d), w_router.T, br)


ISSUE_UNROLL = 8


def _row_copy_wait(src, dst, sem, rows):
    pltpu.make_async_copy(src.at[pl.ds(0, rows)], dst.at[pl.ds(0, rows)], sem).wait()


def _dispatch_kernel(pos_ref, rows_ref, hn_ref, xs_ref, zero_s, sem, zsem, *, tm):
    i = pl.program_id(0)
    m = tm * pl.num_programs(0)
    te = zero_s.shape[0]
    n_tiles = xs_ref.shape[0] // te

    @pl.when(i == 0)
    def _():
        zero_s[...] = jnp.zeros(zero_s.shape, F32)

        def fill(t):
            start = pl.multiple_of(t * te + rows_ref[t] // SUBLANE * SUBLANE, SUBLANE)
            copy = pltpu.make_async_copy(zero_s, xs_ref.at[pl.ds(start, te)], zsem)
            copy.start()
            copy.wait()

        for t in range(n_tiles):
            pl.when(rows_ref[t] < te)(functools.partial(fill, t))

    def body(r, c):
        for k in range(TOP_K):
            dst = pos_ref[k * m + i * tm + r]
            pltpu.make_async_copy(hn_ref.at[pl.ds(r, 1)], xs_ref.at[pl.ds(dst, 1)], sem).start()
        return c

    lax.fori_loop(0, tm, body, 0, unroll=ISSUE_UNROLL)
    for _ in range(TOP_K):
        _row_copy_wait(hn_ref, xs_ref, sem, tm)


def _dispatch(hn, pos, tile_rows, tm_expert):
    m, d = hn.shape
    tm = _tile(m, 512)
    n_sorted = tile_rows.shape[0] * tm_expert
    kern = functools.partial(_dispatch_kernel, tm=tm)
    return pl.pallas_call(
        kern,
        grid_spec=pltpu.PrefetchScalarGridSpec(
            num_scalar_prefetch=2,
            grid=(m // tm,),
            in_specs=[pl.BlockSpec((tm, d), lambda i, pos_ref, rows_ref: (i, 0))],
            out_specs=pl.BlockSpec(memory_space=pl.ANY),
            scratch_shapes=[pltpu.VMEM((tm_expert, d), F32), pltpu.SemaphoreType.DMA(()),
                            pltpu.SemaphoreType.DMA(())]),
        out_shape=jax.ShapeDtypeStruct((n_sorted, d), F32),
        compiler_params=_cparams("arbitrary"),
        name="moe_dispatch",
    )(pos, tile_rows, hn)


def _experts_kernel(te_ref, tr_ref, x_ref, wg_ref, wu_ref, wd_ref, y_ref, xb_s, acc_s, *, row_step):
    del te_ref
    i = pl.program_id(0)
    j = pl.program_id(1)
    n_rows = tr_ref[i]
    valid = n_rows > 0

    @pl.when(valid & (j == 0))
    def _():
        xb_s[...] = x_ref[...].astype(BF16)
        acc_s[...] = jnp.zeros(acc_s.shape, F32)

    def swiglu_rows(rows):
        x = xb_s[0:rows, :]
        gate = _dot(x, wg_ref[0].astype(BF16))
        up = _dot(x, wu_ref[0].astype(BF16))
        act = (jax.nn.silu(gate) * up).astype(BF16)
        acc_s[0:rows, :] += _dot(act, wd_ref[0].astype(BF16))

    tm = xb_s.shape[0]
    for rows in range(row_step, tm + 1, row_step):
        pl.when((n_rows > rows - row_step) & (n_rows <= rows))(functools.partial(swiglu_rows, rows))

    last = j == pl.num_programs(1) - 1

    @pl.when(valid & last)
    def _():
        y_ref[...] = acc_s[...]

    @pl.when(jnp.logical_not(valid) & last)
    def _():
        y_ref[...] = jnp.zeros(y_ref.shape, F32)


def _experts(xs, tile_expert, tile_rows, w_gu, w_down, tm):
    ns, d = xs.shape
    f = w_down.shape[1]
    tf = _tile(f, 512, LANE)
    nf = f // tf
    nt = ns // tm
    row_step = tm // 4 if tm % (4 * MXU_WIDTH) == 0 else tm

    def jj(i, j, tv):
        return jnp.where(tv[i] > 0, j, nf - 1)

    return pl.pallas_call(
        functools.partial(_experts_kernel, row_step=row_step),
        grid_spec=pltpu.PrefetchScalarGridSpec(
            num_scalar_prefetch=2,
            grid=(nt, nf),
            in_specs=[pl.BlockSpec((tm, d), lambda i, j, te, tv: (i, 0)),
                      pl.BlockSpec((1, d, tf), lambda i, j, te, tv: (te[i], 0, jj(i, j, tv))),
                      pl.BlockSpec((1, d, tf), lambda i, j, te, tv: (te[i], 0, nf + jj(i, j, tv))),
                      pl.BlockSpec((1, tf, d), lambda i, j, te, tv: (te[i], jj(i, j, tv), 0))],
            out_specs=pl.BlockSpec((tm, d), lambda i, j, te, tv: (i, 0)),
            scratch_shapes=[pltpu.VMEM((tm, d), BF16), pltpu.VMEM((tm, d), F32)]),
        out_shape=jax.ShapeDtypeStruct((ns, d), F32),
        compiler_params=_cparams("arbitrary", "arbitrary"),
        name="moe_experts",
    )(tile_expert, tile_rows, xs, w_gu, w_gu, w_down)


def _combine_kernel(pos_ref, ys_ref, h_ref, w_ref, p_ref, gp_ref, wpg_ref, wpp_ref, gf_ref, o_ref,
                    buf0_s, buf1_s, sem, *, tm):
    i = pl.program_id(0)
    n = pl.num_programs(0)
    m = tm * n
    bufs = (buf0_s, buf1_s)

    def start_row(step, r, slot):
        for k in range(TOP_K):
            src = pos_ref[k * m + step * tm + r]
            pltpu.make_async_copy(ys_ref.at[pl.ds(src, 1)], bufs[slot].at[k, pl.ds(r, 1)],
                                  sem.at[slot]).start()

    def wait_rows(slot):
        for k in range(TOP_K):
            _row_copy_wait(ys_ref, bufs[slot].at[k], sem.at[slot], tm)

    @pl.when(i == 0)
    def _():
        def body(r, c):
            start_row(0, r, 0)
            return c
        lax.fori_loop(0, tm, body, 0, unroll=ISSUE_UNROLL)

    nxt = jnp.minimum(i + 1, n - 1)

    def step_body(slot):
        wait_rows(slot)
        for r in range(tm):
            start_row(nxt, r, 1 - slot)
        w = w_ref[...]
        h = h_ref[...] + w[:, 0:1] * bufs[slot][0] + w[:, 1:2] * bufs[slot][1]
        h = _ple(h, p_ref, gp_ref, wpg_ref, wpp_ref)
        o_ref[...] = _rms(h, gf_ref[...])

        @pl.when(i == n - 1)
        def _():
            wait_rows(1 - slot)

    for slot in range(2):
        pl.when(lax.rem(i, 2) == slot)(functools.partial(step_body, slot))


def _combine(ys, pos, h, weights, p_layers, layer, g_ple, w_pgate, w_pproj, g_final):
    m, d = h.shape
    pd = p_layers.shape[2]
    tm = _tile(m, 256)
    kern = functools.partial(_combine_kernel, tm=tm)
    row = lambda w: pl.BlockSpec((tm, w), lambda i, pos_ref: (i, 0))
    const = lambda s: pl.BlockSpec(s, lambda i, pos_ref: (0, 0))
    return pl.pallas_call(
        kern,
        grid_spec=pltpu.PrefetchScalarGridSpec(
            num_scalar_prefetch=1,
            grid=(m // tm,),
            in_specs=[pl.BlockSpec(memory_space=pl.ANY), row(d), row(TOP_K),
                      pl.BlockSpec((None, tm, pd), lambda i, pos_ref: (layer, i, 0)), const((1, d)),
                      const((d, d)), const((pd, d)), const((1, d))],
            out_specs=row(d),
            scratch_shapes=[pltpu.VMEM((TOP_K, tm, d), F32), pltpu.VMEM((TOP_K, tm, d), F32),
                            pltpu.SemaphoreType.DMA((2,))]),
        out_shape=jax.ShapeDtypeStruct((m, d), F32),
        compiler_params=_cparams("arbitrary"),
        name="moe_combine",
    )(pos, ys, h, weights, p_layers, g_ple.reshape(1, d), w_pgate, w_pproj, g_final.reshape(1, d))


def _route_plan(route, counts, n_exp, tm, n_tiles):
    e = route[0:TOP_K].astype(jnp.int32)
    rank = route[4:4 + TOP_K].astype(jnp.int32)
    cnt = counts[:, 0].astype(jnp.int32)
    ntile = (cnt + tm - 1) // tm
    tile_end = jnp.cumsum(ntile)
    row_start = (tile_end - ntile) * tm
    sel = e[:, :, None] == jnp.arange(n_exp)[None, None, :]
    pos = jnp.sum(jnp.where(sel, row_start[None, None, :], 0), axis=-1) + rank
    tiles = jnp.arange(n_tiles)
    total = tile_end[-1]
    te = jnp.minimum(jnp.sum(tiles[:, None] >= tile_end[None, :], axis=1), n_exp - 1)
    valid = tiles < total
    te_last = jnp.sum(jnp.where(tiles == total - 1, te, 0))
    te = jnp.where(valid, te, te_last).astype(jnp.int32)
    own = te[:, None] == jnp.arange(n_exp)[None, :]
    tile_in_group = tiles - jnp.sum(jnp.where(own, (tile_end - ntile)[None, :], 0), axis=1)
    group_rows = jnp.sum(jnp.where(own, cnt[None, :], 0), axis=1)
    tile_rows = jnp.where(valid, jnp.clip(group_rows - tile_in_group * tm, 0, tm), 0)
    return pos.reshape(-1).astype(jnp.int32), te, tile_rows.astype(jnp.int32)


def _moe_and_final(attn, h, w_o, g_ffn, w_router, b_router, w_gu, w_down, p_layers, layer, g_ple, w_pgate,
                   w_pproj, g_final, *, tm_expert):
    m, d = h.shape
    n_exp = w_router.shape[1]
    h4, hn, route, counts = _oproj_router(attn, h, w_o, g_ffn, w_router, b_router)
    n_tiles = -(-(m * TOP_K) // tm_expert) + n_exp
    pos, te, tv = _route_plan(route, counts, n_exp, tm_expert, n_tiles)
    xs = _dispatch(hn, pos, tv, tm_expert)
    ys = _experts(xs, te, tv, w_gu, w_down, tm_expert)
    weights = route[2:2 + TOP_K].T
    return _combine(ys, pos, h4, weights, p_layers, layer, g_ple, w_pgate, w_pproj, g_final)


def kernel(x_prompt, x_sample, state_h, state_conv, cache_k, cache_v, cache_logf, page_table, p_prompt, p_sample, g_mix_a, w_in_a, conv_w_a, conv_b_a, w_rg_a, b_rg_a, w_ig_a, b_ig_a, lru_lambda_a, w_out_a, g_kv, w_kv, b_f, g_mix_b, w_q_b, w_o_b, g_ffn, w_gu_dense, w_down_dense, w_router, b_router, w_gu_moe, w_down_moe, g_ple, w_ple_gate, w_ple_proj, g_final):
    assert w_in_a.shape[0] == 1 and w_q_b.shape[0] == 1 and g_ffn.shape[0] == 2
    assert x_sample.shape[1] == 1
    bsz, t_len, d = x_prompt.shape
    dec_b = x_sample.shape[0]
    nh = b_f.shape[0]
    qk = w_q_b.shape[2]
    hd = qk // nh
    r = w_out_a.shape[1]
    scale = hd ** -0.5
    mp = bsz * t_len

    wins, rec_w = _rec_weights(w_in_a[0], conv_w_a[0], conv_b_a[0], w_rg_a[0], b_rg_a[0], w_ig_a[0],
                               b_ig_a[0], lru_lambda_a[0], w_out_a[0])
    w_gu_d = w_gu_dense[0].astype(BF16)
    w_down_d = w_down_dense[0].astype(BF16)
    w_pg = [w_ple_gate[l].astype(BF16) for l in range(2)]
    w_pp = [w_ple_proj[l].astype(BF16) for l in range(2)]

    def rest_of_step(h1, p, seq_len):
        h3 = _ffn_dense(h1, g_ffn[0], w_gu_d, w_down_d, p, 0, g_ple[0], w_pg[0], w_pp[0])
        return h3, _kvq(h3, g_kv, g_mix_b[0], w_kv, b_f, w_q_b[0], seq_len=seq_len)

    def moe(attn, h3, p, tm_expert):
        return _moe_and_final(attn, h3, w_o_b[0], g_ffn[1], w_router[0], b_router[0], w_gu_moe[0],
                              w_down_moe[0], p, 1, g_ple[1], w_pg[1], w_pp[1], g_final, tm_expert=tm_expert)

    h1, hl_p, cn_p = _rec_prompt(x_prompt, g_mix_a[0], wins, rec_w)
    pp = p_prompt.reshape(2, mp, -1)
    h3, (k_p, v_p, q_p, lft_p) = rest_of_step(h1.reshape(mp, d), pp, t_len)
    attn_p = _attn_prompt(q_p.reshape(bsz, t_len, qk), k_p.reshape(bsz, t_len, qk), v_p.reshape(bsz, t_len, qk),
                          lft_p, nh=nh, scale=scale)
    lf_p = jnp.swapaxes(lft_p, 1, 2)
    y_p = moe(attn_p.reshape(mp, qk), h3, pp, min(1024, mp))

    h1s, hl_s, cn_s = _rec_sample(x_sample.reshape(dec_b, d), g_mix_a[0], wins, rec_w, state_conv[0], state_h[0])
    ps = p_sample.reshape(2, dec_b, -1)
    h3s, (k_s, v_s, q_s, lf_s) = rest_of_step(h1s, ps, None)
    attn_s = _attn_sample(q_s.astype(F32), k_s, v_s, lf_s, cache_k, cache_v, cache_logf, page_table, scale=scale)
    y_s = moe(attn_s, h3s, ps, dec_b)

    return (y_p.reshape(bsz, t_len, d), y_s.reshape(dec_b, 1, d),
            hl_p[None], cn_p[None],
            k_p.reshape(bsz, t_len, nh, hd), v_p.reshape(bsz, t_len, nh, hd), lf_p,
            hl_s[None], cn_s[None],
            k_s.reshape(dec_b, 1, nh, hd), v_s.reshape(dec_b, 1, nh, hd), lf_s.reshape(dec_b, 1, nh))
```

```python
import functools

import jax
import jax.numpy as jnp
from jax import lax
from jax.experimental import pallas as pl
from jax.experimental.pallas import tpu as pltpu

F32 = jnp.float32
BF16 = jnp.bfloat16

EPS = 1e-6
LRU_C = 8.0
N_LRU_BLOCKS = 8
CONV_W = 4
N_EXPERTS = 8
TOP_K = 2

LANE = 128
SUBLANE = 8
MXU_WIDTH = 256
VMEM_BYTES = 64 * 1024 * 1024
VMEM_LIMIT = VMEM_BYTES * 7 // 8
NEG_BIG = -1e30
LOG2E = 1.4426950408889634


def _cparams(*sem):
    return pltpu.CompilerParams(dimension_semantics=sem, vmem_limit_bytes=VMEM_LIMIT)


def _tile(n, pref, align=SUBLANE):
    if n <= pref:
        return n
    t = pref // align * align
    while n % t:
        t -= align
    return t


def _full(shape):
    nd = len(shape)
    return pl.BlockSpec(shape, lambda *_: (0,) * nd)


def _rms(x, g):
    ms = jnp.mean(x * x, axis=-1, keepdims=True)
    return x * lax.rsqrt(ms + EPS) * g


def _dot(a, b):
    return jnp.dot(a, b, preferred_element_type=F32)


def _split3_f32(x):
    hi = x.astype(BF16).astype(F32)
    r1 = x - hi
    mid = r1.astype(BF16).astype(F32)
    lo = (r1 - mid).astype(BF16).astype(F32)
    return hi, mid, lo


def _split3(x):
    return tuple(piece.astype(BF16) for piece in _split3_f32(x))


def _band_windows(r, nblk, tw):
    blk = r // nblk
    wins = []
    c0 = 0
    while c0 < r:
        c1 = min(c0 + tw, r)
        b0, b1 = c0 // blk, (c1 - 1) // blk
        k0 = (b0 * blk) // LANE * LANE
        k1 = min(r, -(-((b1 + 1) * blk) // LANE) * LANE)
        wins.append((c0, c1, k0, k1))
        c0 = c1
    return tuple(wins)


def _band_pack(ws, wins, tw):
    parts = []
    dense = [jax.scipy.linalg.block_diag(*w) for w in ws]
    for c0, c1, k0, k1 in wins:
        parts.append(jnp.concatenate(
            [jnp.pad(dn[k0:k1, c0:c1], ((0, 0), (0, tw - (c1 - c0)))) for dn in dense], axis=1))
    return jnp.concatenate(parts, axis=0).astype(BF16)


def _band_matmul(x_bf, w_ref, wins):
    tw = w_ref.shape[1] // 2
    results = []
    for which in range(2):
        outs = []
        off = 0
        for c0, c1, k0, k1 in wins:
            o = _dot(x_bf[:, k0:k1], w_ref[off:off + (k1 - k0), which * tw:(which + 1) * tw])
            outs.append(o[:, :c1 - c0])
            off += k1 - k0
        results.append(jnp.concatenate(outs, axis=1) if len(outs) > 1 else outs[0])
    return tuple(results)


def _lru_gates(uc, rg, ig, lam):
    r = jax.nn.sigmoid(rg)
    i = jax.nn.sigmoid(ig)
    log_a = (-LRU_C) * r * jax.nn.softplus(-lam)
    a = jnp.exp(log_a)
    one_minus_a2 = jnp.tanh(-log_a) * (a * a + 1.0)
    return a, jnp.sqrt(one_minus_a2) * (i * uc)


def _rec_prompt_kernel(x_ref, g_ref, win_ref, cw_ref, cb_ref, wband_ref, brg_ref, big_ref,
                       lam_ref, wout_ref, h_ref, hl_ref, cn_ref, tail_s, a_s, xi_s, hs_s, hc_s,
                       *, wins, tc, r):
    t = pl.program_id(1)

    @pl.when(t == 0)
    def _():
        tail_s[...] = jnp.zeros(tail_s.shape, F32)
        hc_s[...] = jnp.zeros(hc_s.shape, F32)

    x = x_ref[0]
    xn = _rms(x, g_ref[...]).astype(BF16)
    gu = _dot(xn, win_ref[...])
    gate = gu[:, :r]
    u = gu[:, r:]

    tail = tail_s[...]
    row8 = lax.broadcasted_iota(jnp.int32, (SUBLANE, r), 0)
    uc = cb_ref[...]
    for j in range(CONV_W):
        k = CONV_W - 1 - j
        if k:
            sh = pltpu.roll(u, k, 0)
            head = jnp.where(row8 < k, pltpu.roll(tail, k, 0), sh[0:SUBLANE])
            sh = jnp.concatenate([head, sh[SUBLANE:]], axis=0)
        else:
            sh = u
        uc = uc + sh * cw_ref[j:j + 1, :]
    tail_s[...] = u[tc - SUBLANE:tc]

    @pl.when(t == pl.num_programs(1) - 1)
    def _():
        cn_ref[0] = u[tc - (CONV_W - 1):tc]

    rg, ig = _band_matmul(uc.astype(BF16), wband_ref, wins)
    a, xin = _lru_gates(uc, rg + brg_ref[...], ig + big_ref[...], lam_ref[...])
    a_s[...] = a
    xi_s[...] = xin

    row = lax.broadcasted_iota(jnp.int32, (SUBLANE, r), 0)

    def body(gi, h_in):
        off = pl.multiple_of(gi * SUBLANE, SUBLANE)
        a8 = a_s[pl.ds(off, SUBLANE), :]
        x8 = xi_s[pl.ds(off, SUBLANE), :]
        for k in (1, 2, 4):
            m = row >= k
            x8 = x8 + a8 * jnp.where(m, pltpu.roll(x8, k, 0), 0.0)
            a8 = a8 * jnp.where(m, pltpu.roll(a8, k, 0), 1.0)
        h8 = a8 * h_in + x8
        hs_s[pl.ds(off, SUBLANE), :] = h8
        return h8[SUBLANE - 1:SUBLANE, :]

    h_last = lax.fori_loop(0, tc // SUBLANE, body, hc_s[0:1, :])
    hc_s[0:1, :] = h_last
    hl_ref[0] = h_last

    y = (hs_s[...] * jax.nn.gelu(gate)).astype(BF16)
    h_ref[0] = x + _dot(y, wout_ref[...])


def _rec_sample_kernel(x_ref, g_ref, win_ref, cw_ref, cb_ref, wband_ref, brg_ref, big_ref,
                       lam_ref, wout_ref, c0_ref, c1_ref, c2_ref, h0_ref, h_ref, hl_ref, u_ref,
                       *, wins, r):
    x = x_ref[...]
    xn = _rms(x, g_ref[...]).astype(BF16)
    gu = _dot(xn, win_ref[...])
    gate = gu[:, :r]
    u = gu[:, r:]
    uc = cb_ref[...] + c0_ref[...] * cw_ref[0:1, :]
    uc = uc + c1_ref[...] * cw_ref[1:2, :]
    uc = uc + c2_ref[...] * cw_ref[2:3, :]
    uc = uc + u * cw_ref[3:4, :]
    rg, ig = _band_matmul(uc.astype(BF16), wband_ref, wins)
    a, xin = _lru_gates(uc, rg + brg_ref[...], ig + big_ref[...], lam_ref[...])
    h = a * h0_ref[...] + xin
    hl_ref[...] = h
    u_ref[...] = u
    y = (h * jax.nn.gelu(gate)).astype(BF16)
    h_ref[...] = x + _dot(y, wout_ref[...])


def _rec_weights(w_in, conv_w, conv_b, w_rg, b_rg, w_ig, b_ig, lam, w_out):
    r = w_out.shape[0]
    tw = min(MXU_WIDTH, r)
    wins = _band_windows(r, w_rg.shape[0], tw)
    args = (w_in.astype(BF16), conv_w, conv_b.reshape(1, r), _band_pack((w_rg, w_ig), wins, tw),
            b_rg.reshape(1, r), b_ig.reshape(1, r), lam.reshape(1, r), w_out.astype(BF16))
    return wins, args


def _rec_prompt(x, g, wins, wargs):
    b, t, d = x.shape
    r = wargs[-1].shape[0]
    tc = _tile(t, 256)
    kern = functools.partial(_rec_prompt_kernel, wins=wins, tc=tc, r=r)
    wspecs = [_full(a.shape) for a in wargs]
    h, hl, cn = pl.pallas_call(
        kern,
        grid=(b, t // tc),
        in_specs=[pl.BlockSpec((1, tc, d), lambda i, j: (i, j, 0)), _full((1, d))] + wspecs,
        out_specs=[pl.BlockSpec((1, tc, d), lambda i, j: (i, j, 0)),
                   pl.BlockSpec((1, 1, r), lambda i, j: (i, 0, 0)),
                   pl.BlockSpec((1, CONV_W - 1, r), lambda i, j: (i, 0, 0))],
        out_shape=[jax.ShapeDtypeStruct((b, t, d), F32),
                   jax.ShapeDtypeStruct((b, 1, r), F32),
                   jax.ShapeDtypeStruct((b, CONV_W - 1, r), F32)],
        scratch_shapes=[pltpu.VMEM((SUBLANE, r), F32), pltpu.VMEM((tc, r), F32),
                        pltpu.VMEM((tc, r), F32), pltpu.VMEM((tc, r), F32), pltpu.VMEM((SUBLANE, r), F32)],
        compiler_params=_cparams("arbitrary", "arbitrary"),
        name="rec_prompt",
    )(x, g.reshape(1, d), *wargs)
    return h, hl.reshape(b, r), cn


def _rec_sample(x, g, wins, wargs, conv0, h0):
    m, d = x.shape
    r = wargs[-1].shape[0]
    kern = functools.partial(_rec_sample_kernel, wins=wins, r=r)
    h, hl, u = pl.pallas_call(
        kern,
        out_shape=[jax.ShapeDtypeStruct((m, d), F32), jax.ShapeDtypeStruct((m, r), F32),
                   jax.ShapeDtypeStruct((m, r), F32)],
        compiler_params=pltpu.CompilerParams(vmem_limit_bytes=VMEM_LIMIT),
        name="rec_sample",
    )(x, g.reshape(1, d), *wargs, conv0[:, 0], conv0[:, 1], conv0[:, 2], h0)
    conv_new = jnp.stack([conv0[:, 1], conv0[:, 2], u], axis=1)
    return h, hl, conv_new


def _ple(h, p_ref, g_ref, wgate_ref, wproj_ref):
    gate = jax.nn.sigmoid(_dot(_rms(h, g_ref[...]).astype(BF16), wgate_ref[...]))
    return h + gate * _dot(p_ref[...].astype(BF16), wproj_ref[...])


def _ffn_dense_kernel(h_ref, g_ref, wg_ref, wu_ref, wd_ref, p_ref, gp_ref, wpg_ref, wpp_ref,
                      o_ref, hn_s, acc_s):
    j = pl.program_id(1)

    @pl.when(j == 0)
    def _():
        hn_s[...] = _rms(h_ref[...], g_ref[...]).astype(BF16)
        acc_s[...] = jnp.zeros(acc_s.shape, F32)

    hn = hn_s[...]
    act = (jax.nn.silu(_dot(hn, wg_ref[...])) * _dot(hn, wu_ref[...])).astype(BF16)
    acc_s[...] += _dot(act, wd_ref[...])

    @pl.when(j == pl.num_programs(1) - 1)
    def _():
        o_ref[...] = _ple(h_ref[...] + acc_s[...], p_ref, gp_ref, wpg_ref, wpp_ref)


def _ffn_dense(h, g, w_gu, w_down, p_layers, layer, g_ple, w_pgate, w_pproj):
    m, d = h.shape
    f = w_down.shape[0]
    pd = p_layers.shape[2]
    tm = _tile(m, 512)
    tf = _tile(f, 1024, LANE)
    nf = f // tf
    return pl.pallas_call(
        _ffn_dense_kernel,
        grid=(m // tm, nf),
        in_specs=[pl.BlockSpec((tm, d), lambda i, j: (i, 0)), _full((1, d)),
                  pl.BlockSpec((d, tf), lambda i, j: (0, j)),
                  pl.BlockSpec((d, tf), lambda i, j: (0, nf + j)),
                  pl.BlockSpec((tf, d), lambda i, j: (j, 0)),
                  pl.BlockSpec((None, tm, pd), lambda i, j: (layer, i, 0)), _full((1, d)),
                  _full((d, d)), _full((pd, d))],
        out_specs=pl.BlockSpec((tm, d), lambda i, j: (i, 0)),
        out_shape=jax.ShapeDtypeStruct((m, d), F32),
        scratch_shapes=[pltpu.VMEM((tm, d), BF16), pltpu.VMEM((tm, d), F32)],
        compiler_params=_cparams("arbitrary", "arbitrary"),
        name="ffn_dense",
    )(h, g.reshape(1, d), w_gu, w_gu, w_down, p_layers, g_ple.reshape(1, d), w_pgate, w_pproj)


def _kvq_kernel(h_ref, gkv_ref, gq_ref, wkv_ref, bf_ref, wq_ref, k_ref, v_ref, q_ref, lf_ref,
                *, qk, nh, head_major):
    x = h_ref[...]
    xh = x * lax.rsqrt(jnp.mean(x * x, axis=-1, keepdims=True) + EPS)
    kvf = _dot((xh * gkv_ref[...]).astype(BF16), wkv_ref[...])
    k_ref[...] = kvf[:, :qk]
    v_ref[...] = kvf[:, qk:2 * qk]
    lf = jax.nn.log_sigmoid(kvf[:, 2 * qk:] + bf_ref[...])
    if head_major:
        lf_ref[0] = lf.T[:nh, :]
    else:
        lf_ref[...] = lf[:, :nh]
    q_ref[...] = _dot((xh * gq_ref[...]).astype(BF16), wq_ref[...]).astype(BF16)


def _kvq(h, g_kv, g_q, w_kv, b_f, w_q, *, seq_len=None):
    m, d = h.shape
    qk = w_q.shape[1]
    nh = b_f.shape[0]
    head_major = seq_len is not None
    tm = _tile(seq_len if head_major else m, 512, LANE if head_major else SUBLANE)
    wkv = jnp.pad(w_kv, ((0, 0), (0, LANE - nh))).astype(BF16)
    bfp = jnp.pad(b_f, (0, LANE - nh)).reshape(1, LANE)
    out_specs = [pl.BlockSpec((tm, qk), lambda i: (i, 0)), pl.BlockSpec((tm, qk), lambda i: (i, 0)),
                 pl.BlockSpec((tm, qk), lambda i: (i, 0))]
    out_shape = [jax.ShapeDtypeStruct((m, qk), F32), jax.ShapeDtypeStruct((m, qk), F32),
                 jax.ShapeDtypeStruct((m, qk), BF16)]
    if head_major:
        nt = seq_len // tm
        out_specs.append(pl.BlockSpec((1, nh, tm), lambda i: (i // nt, 0, i % nt)))
        out_shape.append(jax.ShapeDtypeStruct((m // seq_len, nh, seq_len), F32))
    else:
        out_specs.append(pl.BlockSpec((tm, nh), lambda i: (i, 0)))
        out_shape.append(jax.ShapeDtypeStruct((m, nh), F32))
    kern = functools.partial(_kvq_kernel, qk=qk, nh=nh, head_major=head_major)
    return pl.pallas_call(
        kern,
        grid=(m // tm,),
        in_specs=[pl.BlockSpec((tm, d), lambda i: (i, 0)), _full((1, d)), _full((1, d)),
                  _full(wkv.shape), _full((1, LANE)), _full((d, qk))],
        out_specs=out_specs,
        out_shape=out_shape,
        compiler_params=_cparams("arbitrary"),
        name="kvq",
    )(h, g_kv.reshape(1, d), g_q.reshape(1, d), wkv, bfp, w_q.astype(BF16))


def _cumsum_lanes(x):
    n = x.shape[1]
    lane = lax.broadcasted_iota(jnp.int32, x.shape, 1)
    k = 1
    while k < n:
        x = x + jnp.where(lane >= k, pltpu.roll(x, k, 1), 0.0)
        k *= 2
    return x


def _fold_lanes(x, op):
    r = x[:, :LANE]
    for c in range(1, x.shape[1] // LANE):
        r = op(r, x[:, c * LANE:(c + 1) * LANE])
    return r


def _attn_prompt_kernel(q_ref, k_ref, v_ref, lft_ref, o_ref, kb_s, vb_s, c2_s, s_s, p_s, *, tq, scale):
    h = pl.program_id(1)
    nblk = c2_s.shape[0]
    kb_s[...] = k_ref[0].astype(BF16)
    vb_s[...] = v_ref[0].astype(BF16)

    @pl.when(h == 0)
    def _():
        c = _cumsum_lanes(lft_ref[0]) * LOG2E
        for j in range(nblk):
            c2_s[j] = c[:, j * tq:(j + 1) * tq]

    c1 = scale * LOG2E
    row = lax.broadcasted_iota(jnp.int32, (tq, tq), 0)
    col = lax.broadcasted_iota(jnp.int32, (tq, tq), 1)
    causal = col <= row

    def scores(n):
        q = q_ref[0, n * tq:(n + 1) * tq, :]
        base = n * (n + 1) // 2
        mrun = jnp.full((tq, LANE), NEG_BIG, F32)
        for kb in range(n + 1):
            s = lax.dot_general(q, kb_s[kb * tq:(kb + 1) * tq, :], (((1,), (1,)), ((), ())),
                                preferred_element_type=F32) * c1
            s = s - c2_s[kb, pl.ds(h, 1), :]
            if kb == n:
                s = jnp.where(causal, s, NEG_BIG)
            s_s[base + kb] = s
            mrun = jnp.maximum(mrun, _fold_lanes(s, jnp.maximum))
        return jnp.max(mrun, axis=1, keepdims=True)

    m = scores(0)
    for n in range(nblk):
        m_next = scores(n + 1) if n + 1 < nblk else None
        base = n * (n + 1) // 2
        lrun = jnp.zeros((tq, LANE), F32)
        for kb in range(n + 1):
            p = jnp.exp2(s_s[base + kb] - m)
            lrun = lrun + _fold_lanes(p, jnp.add)
            p_s[n, :, kb * tq:(kb + 1) * tq] = p.astype(BF16)
        acc = _dot(p_s[n, :, :(n + 1) * tq], vb_s[:(n + 1) * tq, :])
        o_ref[0, n * tq:(n + 1) * tq, :] = (acc / jnp.sum(lrun, axis=1, keepdims=True)).astype(BF16)
        m = m_next


def _attn_prompt(q, k, v, lft, *, nh, scale):
    b, t, qk = q.shape
    hd = qk // nh
    tq = _tile(t, 256, LANE)
    nq = t // tq
    kern = functools.partial(_attn_prompt_kernel, tq=tq, scale=scale)
    head = pl.BlockSpec((1, t, hd), lambda i, h: (i, 0, h))
    return pl.pallas_call(
        kern,
        grid=(b, nh),
        in_specs=[head, head, head, pl.BlockSpec((1, nh, t), lambda i, h: (i, 0, 0))],
        out_specs=head,
        out_shape=jax.ShapeDtypeStruct((b, t, qk), BF16),
        scratch_shapes=[pltpu.VMEM((t, hd), BF16), pltpu.VMEM((t, hd), BF16),
                        pltpu.VMEM((nq, nh, tq), F32), pltpu.VMEM((nq * (nq + 1) // 2, tq, tq), F32),
                        pltpu.VMEM((nq, tq, t), BF16)],
        compiler_params=_cparams("arbitrary", "arbitrary"),
        name="attn_prompt",
    )(q, k, v, lft)


PAGE_BUFFERS = 3


def _attn_sample_kernel(pt_ref, q_ref, kn_ref, vn_ref, lfn_ref, later_ref, ck_ref, cv_ref, clf_ref, o_ref,
                        k_s, v_s, lf_s, m_s, l_s, acc_s, carry_s, sem, *, pp, nh, scale, n_pages):
    p = pl.program_id(1)
    n_steps = pl.num_programs(1)
    g = pl.program_id(0) * n_steps + p
    total = pl.num_programs(0) * n_steps
    ahead = PAGE_BUFFERS - 1

    def page_copies(step, slot):
        seq = step // n_steps
        first = n_pages - 1 - (step - seq * n_steps) * pp
        copies = []
        for j in range(pp):
            page = pt_ref[seq * n_pages + first - j]
            copies.append(pltpu.make_async_copy(ck_ref.at[page], k_s.at[slot, j], sem.at[slot]))
            copies.append(pltpu.make_async_copy(cv_ref.at[page], v_s.at[slot, j], sem.at[slot]))
            copies.append(pltpu.make_async_copy(clf_ref.at[page], lf_s.at[slot, j], sem.at[slot]))
        return copies

    @pl.when(g == 0)
    def _():
        for step in range(ahead):
            @pl.when(step < total)
            def _():
                for c in page_copies(step, step):
                    c.start()

    @pl.when(g + ahead < total)
    def _():
        for c in page_copies(g + ahead, lax.rem(g + ahead, PAGE_BUFFERS)):
            c.start()

    slot = lax.rem(g, PAGE_BUFFERS)
    for c in page_copies(g, slot):
        c.wait()
    k_refs = [k_s.at[slot, j] for j in range(pp)]
    v_refs = [v_s.at[slot, j] for j in range(pp)]
    lf_refs = [lf_s.at[slot, j] for j in range(pp)]
    q = q_ref[0].astype(BF16)

    @pl.when(p == 0)
    def _():
        kn = kn_ref[0].astype(BF16).astype(F32)
        s_new = jnp.sum(q.astype(F32) * kn, axis=1, keepdims=True) * scale
        m_s[...] = jnp.broadcast_to(s_new, m_s.shape)
        l_s[...] = jnp.ones(l_s.shape, F32)
        acc_s[...] = vn_ref[0]
        carry_s[...] = lfn_ref[0]

    scores = [lax.dot_general(q, k_refs[j][...].astype(BF16), (((1,), (1,)), ((), ())),
                              preferred_element_type=F32) for j in range(pp)]

    lfs = [lf_refs[j][...] for j in range(pp)]
    pieces = []
    for lf in lfs:
        pieces.extend(_split3_f32(lf))
    later = _dot(jnp.concatenate(pieces, axis=0).astype(BF16), later_ref[...])
    rows_per_page = later.shape[1]
    row = lax.broadcasted_iota(jnp.int32, (nh, rows_per_page), 0)
    lane = lax.broadcasted_iota(jnp.int32, (nh, rows_per_page), 1)
    own_head = jnp.bitwise_and(lane, nh - 1) == row

    carry = carry_s[...]
    probs = []
    for j in range(pp):
        bias = (later[3 * j * nh:(3 * j + 1) * nh] + later[(3 * j + 1) * nh:(3 * j + 2) * nh]
                + later[(3 * j + 2) * nh:(3 * j + 3) * nh]) + carry[:, 0:1]
        carry = carry + jnp.sum(lfs[j], axis=1, keepdims=True)
        s = jnp.where(own_head, scores[j] * scale + bias, NEG_BIG)
        m_j = jnp.max(s, axis=1, keepdims=True)
        pr = jnp.exp(s - m_j)
        probs.append((m_j, jnp.sum(pr, axis=1, keepdims=True), pr.astype(BF16)))
    carry_s[...] = carry
    parts = [(m_j, l_j, _dot(pr, v_refs[j][...].astype(BF16))) for j, (m_j, l_j, pr) in enumerate(probs)]

    m_old = m_s[...]
    m_new = m_old
    for m_j, _, _ in parts:
        m_new = jnp.maximum(m_new, m_j)
    alpha = jnp.exp(m_old - m_new)
    l_new = alpha * l_s[...]
    acc = alpha * acc_s[...]
    for m_j, l_j, o_j in parts:
        w_j = jnp.exp(m_j - m_new)
        l_new = l_new + w_j * l_j
        acc = acc + w_j * o_j
    m_s[...] = m_new
    l_s[...] = l_new
    acc_s[...] = acc

    @pl.when(p == pl.num_programs(1) - 1)
    def _():
        o_ref[0] = acc_s[...] / l_s[...]


def _attn_sample(q, k_new, v_new, lf_new, cache_k, cache_v, cache_logf, page_table, *, scale):
    b, qk = q.shape
    n_phys, ps, nh, hd = cache_k.shape
    assert ps == LANE and hd == LANE and nh == SUBLANE
    n_pages = page_table.shape[1]
    pp = 8
    while n_pages % pp:
        pp //= 2
    ck = cache_k.reshape(n_phys, ps * nh, hd)
    cv = cache_v.reshape(n_phys, ps * nh, hd)
    clf = jnp.swapaxes(cache_logf, 1, 2)
    pt = page_table.reshape(-1)
    lfn = jnp.broadcast_to(lf_new[:, :, None], (b, nh, LANE))
    later = (jnp.arange(ps)[:, None] > (jnp.arange(ps * nh) // nh)[None, :]).astype(BF16)

    head_spec = pl.BlockSpec((1, nh, hd), lambda i, p, pt_ref: (i, 0, 0))
    hbm = pl.BlockSpec(memory_space=pl.ANY)
    in_specs = [head_spec] * 4 + [pl.BlockSpec((ps, ps * nh), lambda i, p, pt_ref: (0, 0)), hbm, hbm, hbm]
    kern = functools.partial(_attn_sample_kernel, pp=pp, nh=nh, scale=scale, n_pages=n_pages)
    out = pl.pallas_call(
        kern,
        grid_spec=pltpu.PrefetchScalarGridSpec(
            num_scalar_prefetch=1,
            grid=(b, n_pages // pp),
            in_specs=in_specs,
            out_specs=head_spec,
            scratch_shapes=[pltpu.VMEM((PAGE_BUFFERS, pp, ps * nh, hd), F32),
                            pltpu.VMEM((PAGE_BUFFERS, pp, ps * nh, hd), F32),
                            pltpu.VMEM((PAGE_BUFFERS, pp, nh, ps), F32),
                            pltpu.VMEM((nh, LANE), F32), pltpu.VMEM((nh, LANE), F32),
                            pltpu.VMEM((nh, hd), F32), pltpu.VMEM((nh, LANE), F32),
                            pltpu.SemaphoreType.DMA((PAGE_BUFFERS,))]),
        out_shape=jax.ShapeDtypeStruct((b, nh, hd), F32),
        compiler_params=_cparams("arbitrary", "arbitrary"),
        name="attn_sample",
    )(pt, q.reshape(b, nh, hd), k_new.reshape(b, nh, hd), v_new.reshape(b, nh, hd), lfn, later, ck, cv, clf)
    return out.reshape(b, qk)


def _oproj_router_kernel(a_ref, h_ref, wo_ref, g_ref, wr_ref, br_ref, h4_ref, hn_ref, route_ref, cnt_ref,
                         carry_s, *, n_exp):
    i = pl.program_id(0)

    @pl.when(i == 0)
    def _():
        carry_s[...] = jnp.zeros(carry_s.shape, F32)

    h4 = h_ref[...] + _dot(a_ref[...].astype(BF16), wo_ref[...])
    h4_ref[...] = h4
    hn = _rms(h4, g_ref[...])
    hn_ref[...] = hn
    tm = hn.shape[0]

    def dot_nt(a, b):
        return lax.dot_general(a, b, (((1,), (1,)), ((), ())), preferred_element_type=F32)

    x_hi, x_mid, _ = _split3(hn)
    w_hi, w_mid, _ = _split3(wr_ref[...])
    logits = dot_nt(w_hi, x_hi) + dot_nt(w_hi, x_mid) + dot_nt(w_mid, x_hi) + br_ref[:, 0:1]

    exp_id = lax.broadcasted_iota(jnp.int32, (n_exp, tm), 0)
    ex = jnp.exp(logits - jnp.max(logits, axis=0, keepdims=True))
    probs = ex / jnp.sum(ex, axis=0, keepdims=True)
    p1 = jnp.max(probs, axis=0, keepdims=True)
    i1 = jnp.min(jnp.where(probs == p1, exp_id, n_exp), axis=0, keepdims=True)
    probs2 = jnp.where(exp_id == i1, -1.0, probs)
    p2 = jnp.max(probs2, axis=0, keepdims=True)
    i2 = jnp.min(jnp.where(probs2 == p2, exp_id, n_exp), axis=0, keepdims=True)
    den = p1 + p2

    onehot = ((exp_id == i1) | (exp_id == i2)).astype(F32)
    r_i = lax.broadcasted_iota(jnp.int32, (tm, tm), 0)
    c_i = lax.broadcasted_iota(jnp.int32, (tm, tm), 1)
    earlier = (r_i < c_i).astype(BF16)
    ranks = _dot(onehot.astype(BF16), earlier) + carry_s[:, 0:1]
    r1 = jnp.sum(jnp.where(exp_id == i1, ranks, 0.0), axis=0, keepdims=True)
    r2 = jnp.sum(jnp.where(exp_id == i2, ranks, 0.0), axis=0, keepdims=True)
    carry_s[...] = carry_s[...] + jnp.sum(onehot, axis=1, keepdims=True)
    cnt_ref[...] = carry_s[...]

    fields = (i1.astype(F32), i2.astype(F32), p1 / den, p2 / den, r1, r2)
    route = jnp.zeros((n_exp, tm), F32)
    for k, v in enumerate(fields):
        route = jnp.where(exp_id == k, v, route)
    route_ref[...] = route


def _oproj_router(attn, h, w_o, g, w_router, b_router):
    m, d = h.shape
    qk = attn.shape[1]
    n_exp = w_router.shape[1]
    assert n_exp == SUBLANE
    tm = _tile(m, 512, LANE)
    br = jnp.broadcast_to(b_router[:, None], (n_exp, LANE))
    kern = functools.partial(_oproj_router_kernel, n_exp=n_exp)
    row = lambda w: pl.BlockSpec((tm, w), lambda i: (i, 0))
    return pl.pallas_call(
        kern,
        grid=(m // tm,),
        in_specs=[row(qk), row(d), _full((qk, d)), _full((1, d)), _full((n_exp, d)), _full((n_exp, LANE))],
        out_specs=[row(d), row(d), pl.BlockSpec((n_exp, tm), lambda i: (0, i)), _full((n_exp, LANE))],
        out_shape=[jax.ShapeDtypeStruct((m, d), F32), jax.ShapeDtypeStruct((m, d), F32),
                   jax.ShapeDtypeStruct((n_exp, m), F32), jax.ShapeDtypeStruct((n_exp, LANE), F32)],
        scratch_shapes=[pltpu.VMEM((n_exp, LANE), F32)],
        compiler_params=_cparams("arbitrary"),
        name="oproj_router",
    )(attn, h, w_o.astype(BF16), g.reshape(1, d), w_router.T, br)


ISSUE_UNROLL = 8


def _row_copy_wait(src, dst, sem, rows):
    pltpu.make_async_copy(src.at[pl.ds(0, rows)], dst.at[pl.ds(0, rows)], sem).wait()


def _dispatch_kernel(pos_ref, rows_ref, hn_ref, xs_ref, zero_s, sem, zsem, *, tm):
    i = pl.program_id(0)
    m = tm * pl.num_programs(0)
    te = zero_s.shape[0]
    n_tiles = xs_ref.shape[0] // te

    @pl.when(i == 0)
    def _():
        zero_s[...] = jnp.zeros(zero_s.shape, F32)

        def fill(t):
            start = pl.multiple_of(t * te + rows_ref[t] // SUBLANE * SUBLANE, SUBLANE)
            copy = pltpu.make_async_copy(zero_s, xs_ref.at[pl.ds(start, te)], zsem)
            copy.start()
            copy.wait()

        for t in range(n_tiles):
            pl.when(rows_ref[t] < te)(functools.partial(fill, t))

    def body(r, c):
        for k in range(TOP_K):
            dst = pos_ref[k * m + i * tm + r]
            pltpu.make_async_copy(hn_ref.at[pl.ds(r, 1)], xs_ref.at[pl.ds(dst, 1)], sem).start()
        return c

    lax.fori_loop(0, tm, body, 0, unroll=ISSUE_UNROLL)
    for _ in range(TOP_K):
        _row_copy_wait(hn_ref, xs_ref, sem, tm)


def _dispatch(hn, pos, tile_rows, tm_expert):
    m, d = hn.shape
    tm = _tile(m, 512)
    n_sorted = tile_rows.shape[0] * tm_expert
    kern = functools.partial(_dispatch_kernel, tm=tm)
    return pl.pallas_call(
        kern,
        grid_spec=pltpu.PrefetchScalarGridSpec(
            num_scalar_prefetch=2,
            grid=(m // tm,),
            in_specs=[pl.BlockSpec((tm, d), lambda i, pos_ref, rows_ref: (i, 0))],
            out_specs=pl.BlockSpec(memory_space=pl.ANY),
            scratch_shapes=[pltpu.VMEM((tm_expert, d), F32), pltpu.SemaphoreType.DMA(()),
                            pltpu.SemaphoreType.DMA(())]),
        out_shape=jax.ShapeDtypeStruct((n_sorted, d), F32),
        compiler_params=_cparams("arbitrary"),
        name="moe_dispatch",
    )(pos, tile_rows, hn)


def _experts_kernel(te_ref, tr_ref, x_ref, wg_ref, wu_ref, wd_ref, y_ref, xb_s, acc_s, *, row_step):
    del te_ref
    i = pl.program_id(0)
    j = pl.program_id(1)
    n_rows = tr_ref[i]
    valid = n_rows > 0

    @pl.when(valid & (j == 0))
    def _():
        xb_s[...] = x_ref[...].astype(BF16)
        acc_s[...] = jnp.zeros(acc_s.shape, F32)

    def swiglu_rows(rows):
        x = xb_s[0:rows, :]
        gate = _dot(x, wg_ref[0].astype(BF16))
        up = _dot(x, wu_ref[0].astype(BF16))
        act = (jax.nn.silu(gate) * up).astype(BF16)
        acc_s[0:rows, :] += _dot(act, wd_ref[0].astype(BF16))

    tm = xb_s.shape[0]
    for rows in range(row_step, tm + 1, row_step):
        pl.when((n_rows > rows - row_step) & (n_rows <= rows))(functools.partial(swiglu_rows, rows))

    last = j == pl.num_programs(1) - 1

    @pl.when(valid & last)
    def _():
        y_ref[...] = acc_s[...]

    @pl.when(jnp.logical_not(valid) & last)
    def _():
        y_ref[...] = jnp.zeros(y_ref.shape, F32)


def _experts(xs, tile_expert, tile_rows, w_gu, w_down, tm):
    ns, d = xs.shape
    f = w_down.shape[1]
    tf = _tile(f, 512, LANE)
    nf = f // tf
    nt = ns // tm
    row_step = tm // 4 if tm % (4 * MXU_WIDTH) == 0 else tm

    def jj(i, j, tv):
        return jnp.where(tv[i] > 0, j, nf - 1)

    return pl.pallas_call(
        functools.partial(_experts_kernel, row_step=row_step),
        grid_spec=pltpu.PrefetchScalarGridSpec(
            num_scalar_prefetch=2,
            grid=(nt, nf),
            in_specs=[pl.BlockSpec((tm, d), lambda i, j, te, tv: (i, 0)),
                      pl.BlockSpec((1, d, tf), lambda i, j, te, tv: (te[i], 0, jj(i, j, tv))),
                      pl.BlockSpec((1, d, tf), lambda i, j, te, tv: (te[i], 0, nf + jj(i, j, tv))),
                      pl.BlockSpec((1, tf, d), lambda i, j, te, tv: (te[i], jj(i, j, tv), 0))],
            out_specs=pl.BlockSpec((tm, d), lambda i, j, te, tv: (i, 0)),
            scratch_shapes=[pltpu.VMEM((tm, d), BF16), pltpu.VMEM((tm, d), F32)]),
        out_shape=jax.ShapeDtypeStruct((ns, d), F32),
        compiler_params=_cparams("arbitrary", "arbitrary"),
        name="moe_experts",
    )(tile_expert, tile_rows, xs, w_gu, w_gu, w_down)


def _combine_kernel(pos_ref, ys_ref, h_ref, w_ref, p_ref, gp_ref, wpg_ref, wpp_ref, gf_ref, o_ref,
                    buf0_s, buf1_s, sem, *, tm):
    i = pl.program_id(0)
    n = pl.num_programs(0)
    m = tm * n
    bufs = (buf0_s, buf1_s)

    def start_row(step, r, slot):
        for k in range(TOP_K):
            src = pos_ref[k * m + step * tm + r]
            pltpu.make_async_copy(ys_ref.at[pl.ds(src, 1)], bufs[slot].at[k, pl.ds(r, 1)],
                                  sem.at[slot]).start()

    def wait_rows(slot):
        for k in range(TOP_K):
            _row_copy_wait(ys_ref, bufs[slot].at[k], sem.at[slot], tm)

    @pl.when(i == 0)
    def _():
        def body(r, c):
            start_row(0, r, 0)
            return c
        lax.fori_loop(0, tm, body, 0, unroll=ISSUE_UNROLL)

    nxt = jnp.minimum(i + 1, n - 1)

    def step_body(slot):
        wait_rows(slot)
        for r in range(tm):
            start_row(nxt, r, 1 - slot)
        w = w_ref[...]
        h = h_ref[...] + w[:, 0:1] * bufs[slot][0] + w[:, 1:2] * bufs[slot][1]
        h = _ple(h, p_ref, gp_ref, wpg_ref, wpp_ref)
        o_ref[...] = _rms(h, gf_ref[...])

        @pl.when(i == n - 1)
        def _():
            wait_rows(1 - slot)

    for slot in range(2):
        pl.when(lax.rem(i, 2) == slot)(functools.partial(step_body, slot))


def _combine(ys, pos, h, weights, p_layers, layer, g_ple, w_pgate, w_pproj, g_final):
    m, d = h.shape
    pd = p_layers.shape[2]
    tm = _tile(m, 256)
    kern = functools.partial(_combine_kernel, tm=tm)
    row = lambda w: pl.BlockSpec((tm, w), lambda i, pos_ref: (i, 0))
    const = lambda s: pl.BlockSpec(s, lambda i, pos_ref: (0, 0))
    return pl.pallas_call(
        kern,
        grid_spec=pltpu.PrefetchScalarGridSpec(
            num_scalar_prefetch=1,
            grid=(m // tm,),
            in_specs=[pl.BlockSpec(memory_space=pl.ANY), row(d), row(TOP_K),
                      pl.BlockSpec((None, tm, pd), lambda i, pos_ref: (layer, i, 0)), const((1, d)),
                      const((d, d)), const((pd, d)), const((1, d))],
            out_specs=row(d),
            scratch_shapes=[pltpu.VMEM((TOP_K, tm, d), F32), pltpu.VMEM((TOP_K, tm, d), F32),
                            pltpu.SemaphoreType.DMA((2,))]),
        out_shape=jax.ShapeDtypeStruct((m, d), F32),
        compiler_params=_cparams("arbitrary"),
        name="moe_combine",
    )(pos, ys, h, weights, p_layers, g_ple.reshape(1, d), w_pgate, w_pproj, g_final.reshape(1, d))


def _route_plan(route, counts, n_exp, tm, n_tiles):
    e = route[0:TOP_K].astype(jnp.int32)
    rank = route[4:4 + TOP_K].astype(jnp.int32)
    cnt = counts[:, 0].astype(jnp.int32)
    ntile = (cnt + tm - 1) // tm
    tile_end = jnp.cumsum(ntile)
    row_start = (tile_end - ntile) * tm
    sel = e[:, :, None] == jnp.arange(n_exp)[None, None, :]
    pos = jnp.sum(jnp.where(sel, row_start[None, None, :], 0), axis=-1) + rank
    tiles = jnp.arange(n_tiles)
    total = tile_end[-1]
    te = jnp.minimum(jnp.sum(tiles[:, None] >= tile_end[None, :], axis=1), n_exp - 1)
    valid = tiles < total
    te_last = jnp.sum(jnp.where(tiles == total - 1, te, 0))
    te = jnp.where(valid, te, te_last).astype(jnp.int32)
    own = te[:, None] == jnp.arange(n_exp)[None, :]
    tile_in_group = tiles - jnp.sum(jnp.where(own, (tile_end - ntile)[None, :], 0), axis=1)
    group_rows = jnp.sum(jnp.where(own, cnt[None, :], 0), axis=1)
    tile_rows = jnp.where(valid, jnp.clip(group_rows - tile_in_group * tm, 0, tm), 0)
    return pos.reshape(-1).astype(jnp.int32), te, tile_rows.astype(jnp.int32)


def _moe_and_final(attn, h, w_o, g_ffn, w_router, b_router, w_gu, w_down, p_layers, layer, g_ple, w_pgate,
                   w_pproj, g_final, *, tm_expert):
    m, d = h.shape
    n_exp = w_router.shape[1]
    h4, hn, route, counts = _oproj_router(attn, h, w_o, g_ffn, w_router, b_router)
    n_tiles = -(-(m * TOP_K) // tm_expert) + n_exp
    pos, te, tv = _route_plan(route, counts, n_exp, tm_expert, n_tiles)
    xs = _dispatch(hn, pos, tv, tm_expert)
    ys = _experts(xs, te, tv, w_gu, w_down, tm_expert)
    weights = route[2:2 + TOP_K].T
    return _combine(ys, pos, h4, weights, p_layers, layer, g_ple, w_pgate, w_pproj, g_final)


def kernel(x_prompt, x_sample, state_h, state_conv, cache_k, cache_v, cache_logf, page_table, p_prompt, p_sample, g_mix_a, w_in_a, conv_w_a, conv_b_a, w_rg_a, b_rg_a, w_ig_a, b_ig_a, lru_lambda_a, w_out_a, g_kv, w_kv, b_f, g_mix_b, w_q_b, w_o_b, g_ffn, w_gu_dense, w_down_dense, w_router, b_router, w_gu_moe, w_down_moe, g_ple, w_ple_gate, w_ple_proj, g_final):
    assert w_in_a.shape[0] == 1 and w_q_b.shape[0] == 1 and g_ffn.shape[0] == 2
    assert x_sample.shape[1] == 1
    bsz, t_len, d = x_prompt.shape
    dec_b = x_sample.shape[0]
    nh = b_f.shape[0]
    qk = w_q_b.shape[2]
    hd = qk // nh
    r = w_out_a.shape[1]
    scale = hd ** -0.5
    mp = bsz * t_len

    wins, rec_w = _rec_weights(w_in_a[0], conv_w_a[0], conv_b_a[0], w_rg_a[0], b_rg_a[0], w_ig_a[0],
                               b_ig_a[0], lru_lambda_a[0], w_out_a[0])
    w_gu_d = w_gu_dense[0].astype(BF16)
    w_down_d = w_down_dense[0].astype(BF16)
    w_pg = [w_ple_gate[l].astype(BF16) for l in range(2)]
    w_pp = [w_ple_proj[l].astype(BF16) for l in range(2)]

    def rest_of_step(h1, p, seq_len):
        h3 = _ffn_dense(h1, g_ffn[0], w_gu_d, w_down_d, p, 0, g_ple[0], w_pg[0], w_pp[0])
        return h3, _kvq(h3, g_kv, g_mix_b[0], w_kv, b_f, w_q_b[0], seq_len=seq_len)

    def moe(attn, h3, p, tm_expert):
        return _moe_and_final(attn, h3, w_o_b[0], g_ffn[1], w_router[0], b_router[0], w_gu_moe[0],
                              w_down_moe[0], p, 1, g_ple[1], w_pg[1], w_pp[1], g_final, tm_expert=tm_expert)

    h1, hl_p, cn_p = _rec_prompt(x_prompt, g_mix_a[0], wins, rec_w)
    pp = p_prompt.reshape(2, mp, -1)
    h3, (k_p, v_p, q_p, lft_p) = rest_of_step(h1.reshape(mp, d), pp, t_len)
    attn_p = _attn_prompt(q_p.reshape(bsz, t_len, qk), k_p.reshape(bsz, t_len, qk), v_p.reshape(bsz, t_len, qk),
                          lft_p, nh=nh, scale=scale)
    lf_p = jnp.swapaxes(lft_p, 1, 2)
    y_p = moe(attn_p.reshape(mp, qk), h3, pp, min(1024, mp))

    h1s, hl_s, cn_s = _rec_sample(x_sample.reshape(dec_b, d), g_mix_a[0], wins, rec_w, state_conv[0], state_h[0])
    ps = p_sample.reshape(2, dec_b, -1)
    h3s, (k_s, v_s, q_s, lf_s) = rest_of_step(h1s, ps, None)
    attn_s = _attn_sample(q_s.astype(F32), k_s, v_s, lf_s, cache_k, cache_v, cache_logf, page_table, scale=scale)
    y_s = moe(attn_s, h3s, ps, dec_b)

    return (y_p.reshape(bsz, t_len, d), y_s.reshape(dec_b, 1, d),
            hl_p[None], cn_p[None],
            k_p.reshape(bsz, t_len, nh, hd), v_p.reshape(bsz, t_len, nh, hd), lf_p,
            hl_s[None], cn_s[None],
            k_s.reshape(dec_b, 1, nh, hd), v_s.reshape(dec_b, 1, nh, hd), lf_s.reshape(dec_b, 1, nh))
```

```python
import functools

import jax
import jax.numpy as jnp
from jax import lax
from jax.experimental import pallas as pl
from jax.experimental.pallas import tpu as pltpu

F32 = jnp.float32
BF16 = jnp.bfloat16

EPS = 1e-6
LRU_C = 8.0
N_LRU_BLOCKS = 8
CONV_W = 4
N_EXPERTS = 8
TOP_K = 2

LANE = 128
SUBLANE = 8
MXU_WIDTH = 256
VMEM_BYTES = 64 * 1024 * 1024
VMEM_LIMIT = VMEM_BYTES * 7 // 8
NEG_BIG = -1e30
LOG2E = 1.4426950408889634


def _cparams(*sem):
    return pltpu.CompilerParams(dimension_semantics=sem, vmem_limit_bytes=VMEM_LIMIT)


def _tile(n, pref, align=SUBLANE):
    if n <= pref:
        return n
    t = pref // align * align
    while n % t:
        t -= align
    return t


def _full(shape):
    nd = len(shape)
    return pl.BlockSpec(shape, lambda *_: (0,) * nd)


def _rms(x, g):
    ms = jnp.mean(x * x, axis=-1, keepdims=True)
    return x * lax.rsqrt(ms + EPS) * g


def _dot(a, b):
    return jnp.dot(a, b, preferred_element_type=F32)


def _split3_f32(x):
    hi = x.astype(BF16).astype(F32)
    r1 = x - hi
    mid = r1.astype(BF16).astype(F32)
    lo = (r1 - mid).astype(BF16).astype(F32)
    return hi, mid, lo


def _split3(x):
    return tuple(piece.astype(BF16) for piece in _split3_f32(x))


def _band_windows(r, nblk, tw):
    blk = r // nblk
    wins = []
    c0 = 0
    while c0 < r:
        c1 = min(c0 + tw, r)
        b0, b1 = c0 // blk, (c1 - 1) // blk
        k0 = (b0 * blk) // LANE * LANE
        k1 = min(r, -(-((b1 + 1) * blk) // LANE) * LANE)
        wins.append((c0, c1, k0, k1))
        c0 = c1
    return tuple(wins)


def _band_pack(ws, wins, tw):
    parts = []
    dense = [jax.scipy.linalg.block_diag(*w) for w in ws]
    for c0, c1, k0, k1 in wins:
        parts.append(jnp.concatenate(
            [jnp.pad(dn[k0:k1, c0:c1], ((0, 0), (0, tw - (c1 - c0)))) for dn in dense], axis=1))
    return jnp.concatenate(parts, axis=0).astype(BF16)


def _band_matmul(x_bf, w_ref, wins):
    tw = w_ref.shape[1] // 2
    results = []
    for which in range(2):
        outs = []
        off = 0
        for c0, c1, k0, k1 in wins:
            o = _dot(x_bf[:, k0:k1], w_ref[off:off + (k1 - k0), which * tw:(which + 1) * tw])
            outs.append(o[:, :c1 - c0])
            off += k1 - k0
        results.append(jnp.concatenate(outs, axis=1) if len(outs) > 1 else outs[0])
    return tuple(results)


def _lru_gates(uc, rg, ig, lam):
    r = jax.nn.sigmoid(rg)
    i = jax.nn.sigmoid(ig)
    log_a = (-LRU_C) * r * jax.nn.softplus(-lam)
    a = jnp.exp(log_a)
    one_minus_a2 = jnp.tanh(-log_a) * (a * a + 1.0)
    return a, jnp.sqrt(one_minus_a2) * (i * uc)


def _rec_prompt_kernel(x_ref, g_ref, win_ref, cw_ref, cb_ref, wband_ref, brg_ref, big_ref,
                       lam_ref, wout_ref, h_ref, hl_ref, cn_ref, tail_s, a_s, xi_s, hs_s, hc_s,
                       *, wins, tc, r):
    t = pl.program_id(1)

    @pl.when(t == 0)
    def _():
        tail_s[...] = jnp.zeros(tail_s.shape, F32)
        hc_s[...] = jnp.zeros(hc_s.shape, F32)

    x = x_ref[0]
    xn = _rms(x, g_ref[...]).astype(BF16)
    gu = _dot(xn, win_ref[...])
    gate = gu[:, :r]
    u = gu[:, r:]

    tail = tail_s[...]
    row8 = lax.broadcasted_iota(jnp.int32, (SUBLANE, r), 0)
    uc = cb_ref[...]
    for j in range(CONV_W):
        k = CONV_W - 1 - j
        if k:
            sh = pltpu.roll(u, k, 0)
            head = jnp.where(row8 < k, pltpu.roll(tail, k, 0), sh[0:SUBLANE])
            sh = jnp.concatenate([head, sh[SUBLANE:]], axis=0)
        else:
            sh = u
        uc = uc + sh * cw_ref[j:j + 1, :]
    tail_s[...] = u[tc - SUBLANE:tc]

    @pl.when(t == pl.num_programs(1) - 1)
    def _():
        cn_ref[0] = u[tc - (CONV_W - 1):tc]

    rg, ig = _band_matmul(uc.astype(BF16), wband_ref, wins)
    a, xin = _lru_gates(uc, rg + brg_ref[...], ig + big_ref[...], lam_ref[...])
    a_s[...] = a
    xi_s[...] = xin

    row = lax.broadcasted_iota(jnp.int32, (SUBLANE, r), 0)

    def body(gi, h_in):
        off = pl.multiple_of(gi * SUBLANE, SUBLANE)
        a8 = a_s[pl.ds(off, SUBLANE), :]
        x8 = xi_s[pl.ds(off, SUBLANE), :]
        for k in (1, 2, 4):
            m = row >= k
            x8 = x8 + a8 * jnp.where(m, pltpu.roll(x8, k, 0), 0.0)
            a8 = a8 * jnp.where(m, pltpu.roll(a8, k, 0), 1.0)
        h8 = a8 * h_in + x8
        hs_s[pl.ds(off, SUBLANE), :] = h8
        return h8[SUBLANE - 1:SUBLANE, :]

    h_last = lax.fori_loop(0, tc // SUBLANE, body, hc_s[0:1, :])
    hc_s[0:1, :] = h_last
    hl_ref[0] = h_last

    y = (hs_s[...] * jax.nn.gelu(gate)).astype(BF16)
    h_ref[0] = x + _dot(y, wout_ref[...])


def _rec_sample_kernel(x_ref, g_ref, win_ref, cw_ref, cb_ref, wband_ref, brg_ref, big_ref,
                       lam_ref, wout_ref, c0_ref, c1_ref, c2_ref, h0_ref, h_ref, hl_ref, u_ref,
                       *, wins, r):
    x = x_ref[...]
    xn = _rms(x, g_ref[...]).astype(BF16)
    gu = _dot(xn, win_ref[...])
    gate = gu[:, :r]
    u = gu[:, r:]
    uc = cb_ref[...] + c0_ref[...] * cw_ref[0:1, :]
    uc = uc + c1_ref[...] * cw_ref[1:2, :]
    uc = uc + c2_ref[...] * cw_ref[2:3, :]
    uc = uc + u * cw_ref[3:4, :]
    rg, ig = _band_matmul(uc.astype(BF16), wband_ref, wins)
    a, xin = _lru_gates(uc, rg + brg_ref[...], ig + big_ref[...], lam_ref[...])
    h = a * h0_ref[...] + xin
    hl_ref[...] = h
    u_ref[...] = u
    y = (h * jax.nn.gelu(gate)).astype(BF16)
    h_ref[...] = x + _dot(y, wout_ref[...])


def _rec_weights(w_in, conv_w, conv_b, w_rg, b_rg, w_ig, b_ig, lam, w_out):
    r = w_out.shape[0]
    tw = min(MXU_WIDTH, r)
    wins = _band_windows(r, w_rg.shape[0], tw)
    args = (w_in.astype(BF16), conv_w, conv_b.reshape(1, r), _band_pack((w_rg, w_ig), wins, tw),
            b_rg.reshape(1, r), b_ig.reshape(1, r), lam.reshape(1, r), w_out.astype(BF16))
    return wins, args


def _rec_prompt(x, g, wins, wargs):
    b, t, d = x.shape
    r = wargs[-1].shape[0]
    tc = _tile(t, 256)
    kern = functools.partial(_rec_prompt_kernel, wins=wins, tc=tc, r=r)
    wspecs = [_full(a.shape) for a in wargs]
    h, hl, cn = pl.pallas_call(
        kern,
        grid=(b, t // tc),
        in_specs=[pl.BlockSpec((1, tc, d), lambda i, j: (i, j, 0)), _full((1, d))] + wspecs,
        out_specs=[pl.BlockSpec((1, tc, d), lambda i, j: (i, j, 0)),
                   pl.BlockSpec((1, 1, r), lambda i, j: (i, 0, 0)),
                   pl.BlockSpec((1, CONV_W - 1, r), lambda i, j: (i, 0, 0))],
        out_shape=[jax.ShapeDtypeStruct((b, t, d), F32),
                   jax.ShapeDtypeStruct((b, 1, r), F32),
                   jax.ShapeDtypeStruct((b, CONV_W - 1, r), F32)],
        scratch_shapes=[pltpu.VMEM((SUBLANE, r), F32), pltpu.VMEM((tc, r), F32),
                        pltpu.VMEM((tc, r), F32), pltpu.VMEM((tc, r), F32), pltpu.VMEM((SUBLANE, r), F32)],
        compiler_params=_cparams("arbitrary", "arbitrary"),
        name="rec_prompt",
    )(x, g.reshape(1, d), *wargs)
    return h, hl.reshape(b, r), cn


def _rec_sample(x, g, wins, wargs, conv0, h0):
    m, d = x.shape
    r = wargs[-1].shape[0]
    kern = functools.partial(_rec_sample_kernel, wins=wins, r=r)
    h, hl, u = pl.pallas_call(
        kern,
        out_shape=[jax.ShapeDtypeStruct((m, d), F32), jax.ShapeDtypeStruct((m, r), F32),
                   jax.ShapeDtypeStruct((m, r), F32)],
        compiler_params=pltpu.CompilerParams(vmem_limit_bytes=VMEM_LIMIT),
        name="rec_sample",
    )(x, g.reshape(1, d), *wargs, conv0[:, 0], conv0[:, 1], conv0[:, 2], h0)
    conv_new = jnp.stack([conv0[:, 1], conv0[:, 2], u], axis=1)
    return h, hl, conv_new


def _ple(h, p_ref, g_ref, wgate_ref, wproj_ref):
    gate = jax.nn.sigmoid(_dot(_rms(h, g_ref[...]).astype(BF16), wgate_ref[...]))
    return h + gate * _dot(p_ref[...].astype(BF16), wproj_ref[...])


def _ffn_dense_kernel(h_ref, g_ref, wg_ref, wu_ref, wd_ref, p_ref, gp_ref, wpg_ref, wpp_ref,
                      o_ref, hn_s, acc_s):
    j = pl.program_id(1)

    @pl.when(j == 0)
    def _():
        hn_s[...] = _rms(h_ref[...], g_ref[...]).astype(BF16)
        acc_s[...] = jnp.zeros(acc_s.shape, F32)

    hn = hn_s[...]
    act = (jax.nn.silu(_dot(hn, wg_ref[...])) * _dot(hn, wu_ref[...])).astype(BF16)
    acc_s[...] += _dot(act, wd_ref[...])

    @pl.when(j == pl.num_programs(1) - 1)
    def _():
        o_ref[...] = _ple(h_ref[...] + acc_s[...], p_ref, gp_ref, wpg_ref, wpp_ref)


def _ffn_dense(h, g, w_gu, w_down, p_layers, layer, g_ple, w_pgate, w_pproj):
    m, d = h.shape
    f = w_down.shape[0]
    pd = p_layers.shape[2]
    tm = _tile(m, 512)
    tf = _tile(f, 1024, LANE)
    nf = f // tf
    return pl.pallas_call(
        _ffn_dense_kernel,
        grid=(m // tm, nf),
        in_specs=[pl.BlockSpec((tm, d), lambda i, j: (i, 0)), _full((1, d)),
                  pl.BlockSpec((d, tf), lambda i, j: (0, j)),
                  pl.BlockSpec((d, tf), lambda i, j: (0, nf + j)),
                  pl.BlockSpec((tf, d), lambda i, j: (j, 0)),
                  pl.BlockSpec((None, tm, pd), lambda i, j: (layer, i, 0)), _full((1, d)),
                  _full((d, d)), _full((pd, d))],
        out_specs=pl.BlockSpec((tm, d), lambda i, j: (i, 0)),
        out_shape=jax.ShapeDtypeStruct((m, d), F32),
        scratch_shapes=[pltpu.VMEM((tm, d), BF16), pltpu.VMEM((tm, d), F32)],
        compiler_params=_cparams("arbitrary", "arbitrary"),
        name="ffn_dense",
    )(h, g.reshape(1, d), w_gu, w_gu, w_down, p_layers, g_ple.reshape(1, d), w_pgate, w_pproj)


def _kvq_kernel(h_ref, gkv_ref, gq_ref, wkv_ref, bf_ref, wq_ref, k_ref, v_ref, q_ref, lf_ref,
                *, qk, nh, head_major):
    x = h_ref[...]
    xh = x * lax.rsqrt(jnp.mean(x * x, axis=-1, keepdims=True) + EPS)
    kvf = _dot((xh * gkv_ref[...]).astype(BF16), wkv_ref[...])
    k_ref[...] = kvf[:, :qk]
    v_ref[...] = kvf[:, qk:2 * qk]
    lf = jax.nn.log_sigmoid(kvf[:, 2 * qk:] + bf_ref[...])
    if head_major:
        lf_ref[0] = lf.T[:nh, :]
    else:
        lf_ref[...] = lf[:, :nh]
    q_ref[...] = _dot((xh * gq_ref[...]).astype(BF16), wq_ref[...]).astype(BF16)


def _kvq(h, g_kv, g_q, w_kv, b_f, w_q, *, seq_len=None):
    m, d = h.shape
    qk = w_q.shape[1]
    nh = b_f.shape[0]
    head_major = seq_len is not None
    tm = _tile(seq_len if head_major else m, 512, LANE if head_major else SUBLANE)
    wkv = jnp.pad(w_kv, ((0, 0), (0, LANE - nh))).astype(BF16)
    bfp = jnp.pad(b_f, (0, LANE - nh)).reshape(1, LANE)
    out_specs = [pl.BlockSpec((tm, qk), lambda i: (i, 0)), pl.BlockSpec((tm, qk), lambda i: (i, 0)),
                 pl.BlockSpec((tm, qk), lambda i: (i, 0))]
    out_shape = [jax.ShapeDtypeStruct((m, qk), F32), jax.ShapeDtypeStruct((m, qk), F32),
                 jax.ShapeDtypeStruct((m, qk), BF16)]
    if head_major:
        nt = seq_len // tm
        out_specs.append(pl.BlockSpec((1, nh, tm), lambda i: (i // nt, 0, i % nt)))
        out_shape.append(jax.ShapeDtypeStruct((m // seq_len, nh, seq_len), F32))
    else:
        out_specs.append(pl.BlockSpec((tm, nh), lambda i: (i, 0)))
        out_shape.append(jax.ShapeDtypeStruct((m, nh), F32))
    kern = functools.partial(_kvq_kernel, qk=qk, nh=nh, head_major=head_major)
    return pl.pallas_call(
        kern,
        grid=(m // tm,),
        in_specs=[pl.BlockSpec((tm, d), lambda i: (i, 0)), _full((1, d)), _full((1, d)),
                  _full(wkv.shape), _full((1, LANE)), _full((d, qk))],
        out_specs=out_specs,
        out_shape=out_shape,
        compiler_params=_cparams("arbitrary"),
        name="kvq",
    )(h, g_kv.reshape(1, d), g_q.reshape(1, d), wkv, bfp, w_q.astype(BF16))


def _cumsum_lanes(x):
    n = x.shape[1]
    lane = lax.broadcasted_iota(jnp.int32, x.shape, 1)
    k = 1
    while k < n:
        x = x + jnp.where(lane >= k, pltpu.roll(x, k, 1), 0.0)
        k *= 2
    return x


def _fold_lanes(x, op):
    r = x[:, :LANE]
    for c in range(1, x.shape[1] // LANE):
        r = op(r, x[:, c * LANE:(c + 1) * LANE])
    return r


def _attn_prompt_kernel(q_ref, k_ref, v_ref, lft_ref, o_ref, kb_s, vb_s, c2_s, s_s, p_s, *, tq, scale):
    h = pl.program_id(1)
    nblk = c2_s.shape[0]
    kb_s[...] = k_ref[0].astype(BF16)
    vb_s[...] = v_ref[0].astype(BF16)

    @pl.when(h == 0)
    def _():
        c = _cumsum_lanes(lft_ref[0]) * LOG2E
        for j in range(nblk):
            c2_s[j] = c[:, j * tq:(j + 1) * tq]

    c1 = scale * LOG2E
    row = lax.broadcasted_iota(jnp.int32, (tq, tq), 0)
    col = lax.broadcasted_iota(jnp.int32, (tq, tq), 1)
    causal = col <= row

    def scores(n):
        q = q_ref[0, n * tq:(n + 1) * tq, :]
        base = n * (n + 1) // 2
        mrun = jnp.full((tq, LANE), NEG_BIG, F32)
        for kb in range(n + 1):
            s = lax.dot_general(q, kb_s[kb * tq:(kb + 1) * tq, :], (((1,), (1,)), ((), ())),
                                preferred_element_type=F32) * c1
            s = s - c2_s[kb, pl.ds(h, 1), :]
            if kb == n:
                s = jnp.where(causal, s, NEG_BIG)
            s_s[base + kb] = s
            mrun = jnp.maximum(mrun, _fold_lanes(s, jnp.maximum))
        return jnp.max(mrun, axis=1, keepdims=True)

    m = scores(0)
    for n in range(nblk):
        m_next = scores(n + 1) if n + 1 < nblk else None
        base = n * (n + 1) // 2
        lrun = jnp.zeros((tq, LANE), F32)
        for kb in range(n + 1):
            p = jnp.exp2(s_s[base + kb] - m)
            lrun = lrun + _fold_lanes(p, jnp.add)
            p_s[n, :, kb * tq:(kb + 1) * tq] = p.astype(BF16)
        acc = _dot(p_s[n, :, :(n + 1) * tq], vb_s[:(n + 1) * tq, :])
        o_ref[0, n * tq:(n + 1) * tq, :] = (acc / jnp.sum(lrun, axis=1, keepdims=True)).astype(BF16)
        m = m_next


def _attn_prompt(q, k, v, lft, *, nh, scale):
    b, t, qk = q.shape
    hd = qk // nh
    tq = _tile(t, 256, LANE)
    nq = t // tq
    kern = functools.partial(_attn_prompt_kernel, tq=tq, scale=scale)
    head = pl.BlockSpec((1, t, hd), lambda i, h: (i, 0, h))
    return pl.pallas_call(
        kern,
        grid=(b, nh),
        in_specs=[head, head, head, pl.BlockSpec((1, nh, t), lambda i, h: (i, 0, 0))],
        out_specs=head,
        out_shape=jax.ShapeDtypeStruct((b, t, qk), BF16),
        scratch_shapes=[pltpu.VMEM((t, hd), BF16), pltpu.VMEM((t, hd), BF16),
                        pltpu.VMEM((nq, nh, tq), F32), pltpu.VMEM((nq * (nq + 1) // 2, tq, tq), F32),
                        pltpu.VMEM((nq, tq, t), BF16)],
        compiler_params=_cparams("arbitrary", "arbitrary"),
        name="attn_prompt",
    )(q, k, v, lft)


PAGE_BUFFERS = 3


def _attn_sample_kernel(pt_ref, q_ref, kn_ref, vn_ref, lfn_ref, later_ref, ck_ref, cv_ref, clf_ref, o_ref,
                        k_s, v_s, lf_s, m_s, l_s, acc_s, carry_s, sem, *, pp, nh, scale, n_pages):
    p = pl.program_id(1)
    n_steps = pl.num_programs(1)
    g = pl.program_id(0) * n_steps + p
    total = pl.num_programs(0) * n_steps
    ahead = PAGE_BUFFERS - 1

    def page_copies(step, slot):
        seq = step // n_steps
        first = n_pages - 1 - (step - seq * n_steps) * pp
        copies = []
        for j in range(pp):
            page = pt_ref[seq * n_pages + first - j]
            copies.append(pltpu.make_async_copy(ck_ref.at[page], k_s.at[slot, j], sem.at[slot]))
            copies.append(pltpu.make_async_copy(cv_ref.at[page], v_s.at[slot, j], sem.at[slot]))
            copies.append(pltpu.make_async_copy(clf_ref.at[page], lf_s.at[slot, j], sem.at[slot]))
        return copies

    @pl.when(g == 0)
    def _():
        for step in range(ahead):
            @pl.when(step < total)
            def _():
                for c in page_copies(step, step):
                    c.start()

    @pl.when(g + ahead < total)
    def _():
        for c in page_copies(g + ahead, lax.rem(g + ahead, PAGE_BUFFERS)):
            c.start()

    slot = lax.rem(g, PAGE_BUFFERS)
    for c in page_copies(g, slot):
        c.wait()
    k_refs = [k_s.at[slot, j] for j in range(pp)]
    v_refs = [v_s.at[slot, j] for j in range(pp)]
    lf_refs = [lf_s.at[slot, j] for j in range(pp)]
    q = q_ref[0].astype(BF16)

    @pl.when(p == 0)
    def _():
        kn = kn_ref[0].astype(BF16).astype(F32)
        s_new = jnp.sum(q.astype(F32) * kn, axis=1, keepdims=True) * scale
        m_s[...] = jnp.broadcast_to(s_new, m_s.shape)
        l_s[...] = jnp.ones(l_s.shape, F32)
        acc_s[...] = vn_ref[0]
        carry_s[...] = lfn_ref[0]

    scores = [lax.dot_general(q, k_refs[j][...].astype(BF16), (((1,), (1,)), ((), ())),
                              preferred_element_type=F32) for j in range(pp)]

    lfs = [lf_refs[j][...] for j in range(pp)]
    pieces = []
    for lf in lfs:
        pieces.extend(_split3_f32(lf))
    later = _dot(jnp.concatenate(pieces, axis=0).astype(BF16), later_ref[...])
    rows_per_page = later.shape[1]
    row = lax.broadcasted_iota(jnp.int32, (nh, rows_per_page), 0)
    lane = lax.broadcasted_iota(jnp.int32, (nh, rows_per_page), 1)
    own_head = jnp.bitwise_and(lane, nh - 1) == row

    carry = carry_s[...]
    probs = []
    for j in range(pp):
        bias = (later[3 * j * nh:(3 * j + 1) * nh] + later[(3 * j + 1) * nh:(3 * j + 2) * nh]
                + later[(3 * j + 2) * nh:(3 * j + 3) * nh]) + carry[:, 0:1]
        carry = carry + jnp.sum(lfs[j], axis=1, keepdims=True)
        s = jnp.where(own_head, scores[j] * scale + bias, NEG_BIG)
        m_j = jnp.max(s, axis=1, keepdims=True)
        pr = jnp.exp(s - m_j)
        probs.append((m_j, jnp.sum(pr, axis=1, keepdims=True), pr.astype(BF16)))
    carry_s[...] = carry
    parts = [(m_j, l_j, _dot(pr, v_refs[j][...].astype(BF16))) for j, (m_j, l_j, pr) in enumerate(probs)]

    m_old = m_s[...]
    m_new = m_old
    for m_j, _, _ in parts:
        m_new = jnp.maximum(m_new, m_j)
    alpha = jnp.exp(m_old - m_new)
    l_new = alpha * l_s[...]
    acc = alpha * acc_s[...]
    for m_j, l_j, o_j in parts:
        w_j = jnp.exp(m_j - m_new)
        l_new = l_new + w_j * l_j
        acc = acc + w_j * o_j
    m_s[...] = m_new
    l_s[...] = l_new
    acc_s[...] = acc

    @pl.when(p == pl.num_programs(1) - 1)
    def _():
        o_ref[0] = acc_s[...] / l_s[...]


def _attn_sample(q, k_new, v_new, lf_new, cache_k, cache_v, cache_logf, page_table, *, scale):
    b, qk = q.shape
    n_phys, ps, nh, hd = cache_k.shape
    assert ps == LANE and hd == LANE and nh == SUBLANE
    n_pages = page_table.shape[1]
    pp = 8
    while n_pages % pp:
        pp //= 2
    ck = cache_k.reshape(n_phys, ps * nh, hd)
    cv = cache_v.reshape(n_phys, ps * nh, hd)
    clf = jnp.swapaxes(cache_logf, 1, 2)
    pt = page_table.reshape(-1)
    lfn = jnp.broadcast_to(lf_new[:, :, None], (b, nh, LANE))
    later = (jnp.arange(ps)[:, None] > (jnp.arange(ps * nh) // nh)[None, :]).astype(BF16)

    head_spec = pl.BlockSpec((1, nh, hd), lambda i, p, pt_ref: (i, 0, 0))
    hbm = pl.BlockSpec(memory_space=pl.ANY)
    in_specs = [head_spec] * 4 + [pl.BlockSpec((ps, ps * nh), lambda i, p, pt_ref: (0, 0)), hbm, hbm, hbm]
    kern = functools.partial(_attn_sample_kernel, pp=pp, nh=nh, scale=scale, n_pages=n_pages)
    out = pl.pallas_call(
        kern,
        grid_spec=pltpu.PrefetchScalarGridSpec(
            num_scalar_prefetch=1,
            grid=(b, n_pages // pp),
            in_specs=in_specs,
            out_specs=head_spec,
            scratch_shapes=[pltpu.VMEM((PAGE_BUFFERS, pp, ps * nh, hd), F32),
                            pltpu.VMEM((PAGE_BUFFERS, pp, ps * nh, hd), F32),
                            pltpu.VMEM((PAGE_BUFFERS, pp, nh, ps), F32),
                            pltpu.VMEM((nh, LANE), F32), pltpu.VMEM((nh, LANE), F32),
                            pltpu.VMEM((nh, hd), F32), pltpu.VMEM((nh, LANE), F32),
                            pltpu.SemaphoreType.DMA((PAGE_BUFFERS,))]),
        out_shape=jax.ShapeDtypeStruct((b, nh, hd), F32),
        compiler_params=_cparams("arbitrary", "arbitrary"),
        name="attn_sample",
    )(pt, q.reshape(b, nh, hd), k_new.reshape(b, nh, hd), v_new.reshape(b, nh, hd), lfn, later, ck, cv, clf)
    return out.reshape(b, qk)


def _oproj_router_kernel(a_ref, h_ref, wo_ref, g_ref, wr_ref, br_ref, h4_ref, hn_ref, route_ref, cnt_ref,
                         carry_s, *, n_exp):
    i = pl.program_id(0)

    @pl.when(i == 0)
    def _():
        carry_s[...] = jnp.zeros(carry_s.shape, F32)

    h4 = h_ref[...] + _dot(a_ref[...].astype(BF16), wo_ref[...])
    h4_ref[...] = h4
    hn = _rms(h4, g_ref[...])
    hn_ref[...] = hn
    tm = hn.shape[0]

    def dot_nt(a, b):
        return lax.dot_general(a, b, (((1,), (1,)), ((), ())), preferred_element_type=F32)

    x_hi, x_mid, _ = _split3(hn)
    w_hi, w_mid, _ = _split3(wr_ref[...])
    logits = dot_nt(w_hi, x_hi) + dot_nt(w_hi, x_mid) + dot_nt(w_mid, x_hi) + br_ref[:, 0:1]

    exp_id = lax.broadcasted_iota(jnp.int32, (n_exp, tm), 0)
    ex = jnp.exp(logits - jnp.max(logits, axis=0, keepdims=True))
    probs = ex / jnp.sum(ex, axis=0, keepdims=True)
    p1 = jnp.max(probs, axis=0, keepdims=True)
    i1 = jnp.min(jnp.where(probs == p1, exp_id, n_exp), axis=0, keepdims=True)
    probs2 = jnp.where(exp_id == i1, -1.0, probs)
    p2 = jnp.max(probs2, axis=0, keepdims=True)
    i2 = jnp.min(jnp.where(probs2 == p2, exp_id, n_exp), axis=0, keepdims=True)
    den = p1 + p2

    onehot = ((exp_id == i1) | (exp_id == i2)).astype(F32)
    r_i = lax.broadcasted_iota(jnp.int32, (tm, tm), 0)
    c_i = lax.broadcasted_iota(jnp.int32, (tm, tm), 1)
    earlier = (r_i < c_i).astype(BF16)
    ranks = _dot(onehot.astype(BF16), earlier) + carry_s[:, 0:1]
    r1 = jnp.sum(jnp.where(exp_id == i1, ranks, 0.0), axis=0, keepdims=True)
    r2 = jnp.sum(jnp.where(exp_id == i2, ranks, 0.0), axis=0, keepdims=True)
    carry_s[...] = carry_s[...] + jnp.sum(onehot, axis=1, keepdims=True)
    cnt_ref[...] = carry_s[...]

    fields = (i1.astype(F32), i2.astype(F32), p1 / den, p2 / den, r1, r2)
    route = jnp.zeros((n_exp, tm), F32)
    for k, v in enumerate(fields):
        route = jnp.where(exp_id == k, v, route)
    route_ref[...] = route


def _oproj_router(attn, h, w_o, g, w_router, b_router):
    m, d = h.shape
    qk = attn.shape[1]
    n_exp = w_router.shape[1]
    assert n_exp == SUBLANE
    tm = _tile(m, 512, LANE)
    br = jnp.broadcast_to(b_router[:, None], (n_exp, LANE))
    kern = functools.partial(_oproj_router_kernel, n_exp=n_exp)
    row = lambda w: pl.BlockSpec((tm, w), lambda i: (i, 0))
    return pl.pallas_call(
        kern,
        grid=(m // tm,),
        in_specs=[row(qk), row(d), _full((qk, d)), _full((1, d)), _full((n_exp, d)), _full((n_exp, LANE))],
        out_specs=[row(d), row(d), pl.BlockSpec((n_exp, tm), lambda i: (0, i)), _full((n_exp, LANE))],
        out_shape=[jax.ShapeDtypeStruct((m, d), F32), jax.ShapeDtypeStruct((m, d), F32),
                   jax.ShapeDtypeStruct((n_exp, m), F32), jax.ShapeDtypeStruct((n_exp, LANE), F32)],
        scratch_shapes=[pltpu.VMEM((n_exp, LANE), F32)],
        compiler_params=_cparams("arbitrary"),
        name="oproj_router",
    )(attn, h, w_o.astype(BF16), g.reshape(1, d), w_router.T, br)


ISSUE_UNROLL = 8


def _row_copy_wait(src, dst, sem, rows):
    pltpu.make_async_copy(src.at[pl.ds(0, rows)], dst.at[pl.ds(0, rows)], sem).wait()


def _dispatch_kernel(pos_ref, rows_ref, hn_ref, xs_ref, zero_s, sem, zsem, *, tm):
    i = pl.program_id(0)
    m = tm * pl.num_programs(0)
    te = zero_s.shape[0]
    n_tiles = xs_ref.shape[0] // te

    @pl.when(i == 0)
    def _():
        zero_s[...] = jnp.zeros(zero_s.shape, F32)

        def fill(t):
            start = pl.multiple_of(t * te + rows_ref[t] // SUBLANE * SUBLANE, SUBLANE)
            copy = pltpu.make_async_copy(zero_s, xs_ref.at[pl.ds(start, te)], zsem)
            copy.start()
            copy.wait()

        for t in range(n_tiles):
            pl.when(rows_ref[t] < te)(functools.partial(fill, t))

    def body(r, c):
        for k in range(TOP_K):
            dst = pos_ref[k * m + i * tm + r]
            pltpu.make_async_copy(hn_ref.at[pl.ds(r, 1)], xs_ref.at[pl.ds(dst, 1)], sem).start(priority=k)
        return c

    lax.fori_loop(0, tm, body, 0, unroll=ISSUE_UNROLL)
    for _ in range(TOP_K):
        _row_copy_wait(hn_ref, xs_ref, sem, tm)


def _dispatch(hn, pos, tile_rows, tm_expert):
    m, d = hn.shape
    tm = _tile(m, 512)
    n_sorted = tile_rows.shape[0] * tm_expert
    kern = functools.partial(_dispatch_kernel, tm=tm)
    return pl.pallas_call(
        kern,
        grid_spec=pltpu.PrefetchScalarGridSpec(
            num_scalar_prefetch=2,
            grid=(m // tm,),
            in_specs=[pl.BlockSpec((tm, d), lambda i, pos_ref, rows_ref: (i, 0))],
            out_specs=pl.BlockSpec(memory_space=pl.ANY),
            scratch_shapes=[pltpu.VMEM((tm_expert, d), F32), pltpu.SemaphoreType.DMA(()),
                            pltpu.SemaphoreType.DMA(())]),
        out_shape=jax.ShapeDtypeStruct((n_sorted, d), F32),
        compiler_params=_cparams("arbitrary"),
        name="moe_dispatch",
    )(pos, tile_rows, hn)


def _experts_kernel(te_ref, tr_ref, x_ref, wg_ref, wu_ref, wd_ref, y_ref, xb_s, acc_s, *, row_step):
    del te_ref
    i = pl.program_id(0)
    j = pl.program_id(1)
    n_rows = tr_ref[i]
    valid = n_rows > 0

    @pl.when(valid & (j == 0))
    def _():
        xb_s[...] = x_ref[...].astype(BF16)
        acc_s[...] = jnp.zeros(acc_s.shape, F32)

    def swiglu_rows(rows):
        x = xb_s[0:rows, :]
        gate = _dot(x, wg_ref[0].astype(BF16))
        up = _dot(x, wu_ref[0].astype(BF16))
        act = (jax.nn.silu(gate) * up).astype(BF16)
        acc_s[0:rows, :] += _dot(act, wd_ref[0].astype(BF16))

    tm = xb_s.shape[0]
    for rows in range(row_step, tm + 1, row_step):
        pl.when((n_rows > rows - row_step) & (n_rows <= rows))(functools.partial(swiglu_rows, rows))

    last = j == pl.num_programs(1) - 1

    @pl.when(valid & last)
    def _():
        y_ref[...] = acc_s[...]

    @pl.when(jnp.logical_not(valid) & last)
    def _():
        y_ref[...] = jnp.zeros(y_ref.shape, F32)


def _experts(xs, tile_expert, tile_rows, w_gu, w_down, tm):
    ns, d = xs.shape
    f = w_down.shape[1]
    tf = _tile(f, 512, LANE)
    nf = f // tf
    nt = ns // tm
    row_step = tm // 4 if tm % (4 * MXU_WIDTH) == 0 else tm

    def jj(i, j, tv):
        return jnp.where(tv[i] > 0, j, nf - 1)

    return pl.pallas_call(
        functools.partial(_experts_kernel, row_step=row_step),
        grid_spec=pltpu.PrefetchScalarGridSpec(
            num_scalar_prefetch=2,
            grid=(nt, nf),
            in_specs=[pl.BlockSpec((tm, d), lambda i, j, te, tv: (i, 0)),
                      pl.BlockSpec((1, d, tf), lambda i, j, te, tv: (te[i], 0, jj(i, j, tv))),
                      pl.BlockSpec((1, d, tf), lambda i, j, te, tv: (te[i], 0, nf + jj(i, j, tv))),
                      pl.BlockSpec((1, tf, d), lambda i, j, te, tv: (te[i], jj(i, j, tv), 0))],
            out_specs=pl.BlockSpec((tm, d), lambda i, j, te, tv: (i, 0)),
            scratch_shapes=[pltpu.VMEM((tm, d), BF16), pltpu.VMEM((tm, d), F32)]),
        out_shape=jax.ShapeDtypeStruct((ns, d), F32),
        compiler_params=_cparams("arbitrary", "arbitrary"),
        name="moe_experts",
    )(tile_expert, tile_rows, xs, w_gu, w_gu, w_down)


def _combine_kernel(pos_ref, ys_ref, h_ref, w_ref, p_ref, gp_ref, wpg_ref, wpp_ref, gf_ref, o_ref,
                    buf0_s, buf1_s, sem, *, tm):
    i = pl.program_id(0)
    n = pl.num_programs(0)
    m = tm * n
    bufs = (buf0_s, buf1_s)

    def start_row(step, r, slot):
        for k in range(TOP_K):
            src = pos_ref[k * m + step * tm + r]
            pltpu.make_async_copy(ys_ref.at[pl.ds(src, 1)], bufs[slot].at[k, pl.ds(r, 1)],
                                  sem.at[slot]).start(priority=k)

    def wait_rows(slot):
        for k in range(TOP_K):
            _row_copy_wait(ys_ref, bufs[slot].at[k], sem.at[slot], tm)

    @pl.when(i == 0)
    def _():
        def body(r, c):
            start_row(0, r, 0)
            return c
        lax.fori_loop(0, tm, body, 0, unroll=ISSUE_UNROLL)

    nxt = jnp.minimum(i + 1, n - 1)

    def step_body(slot):
        wait_rows(slot)
        for r in range(tm):
            start_row(nxt, r, 1 - slot)
        w = w_ref[...]
        h = h_ref[...] + w[:, 0:1] * bufs[slot][0] + w[:, 1:2] * bufs[slot][1]
        h = _ple(h, p_ref, gp_ref, wpg_ref, wpp_ref)
        o_ref[...] = _rms(h, gf_ref[...])

        @pl.when(i == n - 1)
        def _():
            wait_rows(1 - slot)

    for slot in range(2):
        pl.when(lax.rem(i, 2) == slot)(functools.partial(step_body, slot))


def _combine(ys, pos, h, weights, p_layers, layer, g_ple, w_pgate, w_pproj, g_final):
    m, d = h.shape
    pd = p_layers.shape[2]
    tm = _tile(m, 256)
    kern = functools.partial(_combine_kernel, tm=tm)
    row = lambda w: pl.BlockSpec((tm, w), lambda i, pos_ref: (i, 0))
    const = lambda s: pl.BlockSpec(s, lambda i, pos_ref: (0, 0))
    return pl.pallas_call(
        kern,
        grid_spec=pltpu.PrefetchScalarGridSpec(
            num_scalar_prefetch=1,
            grid=(m // tm,),
            in_specs=[pl.BlockSpec(memory_space=pl.ANY), row(d), row(TOP_K),
                      pl.BlockSpec((None, tm, pd), lambda i, pos_ref: (layer, i, 0)), const((1, d)),
                      const((d, d)), const((pd, d)), const((1, d))],
            out_specs=row(d),
            scratch_shapes=[pltpu.VMEM((TOP_K, tm, d), F32), pltpu.VMEM((TOP_K, tm, d), F32),
                            pltpu.SemaphoreType.DMA((2,))]),
        out_shape=jax.ShapeDtypeStruct((m, d), F32),
        compiler_params=_cparams("arbitrary"),
        name="moe_combine",
    )(pos, ys, h, weights, p_layers, g_ple.reshape(1, d), w_pgate, w_pproj, g_final.reshape(1, d))


def _route_plan(route, counts, n_exp, tm, n_tiles):
    e = route[0:TOP_K].astype(jnp.int32)
    rank = route[4:4 + TOP_K].astype(jnp.int32)
    cnt = counts[:, 0].astype(jnp.int32)
    ntile = (cnt + tm - 1) // tm
    tile_end = jnp.cumsum(ntile)
    row_start = (tile_end - ntile) * tm
    sel = e[:, :, None] == jnp.arange(n_exp)[None, None, :]
    pos = jnp.sum(jnp.where(sel, row_start[None, None, :], 0), axis=-1) + rank
    tiles = jnp.arange(n_tiles)
    total = tile_end[-1]
    te = jnp.minimum(jnp.sum(tiles[:, None] >= tile_end[None, :], axis=1), n_exp - 1)
    valid = tiles < total
    te_last = jnp.sum(jnp.where(tiles == total - 1, te, 0))
    te = jnp.where(valid, te, te_last).astype(jnp.int32)
    own = te[:, None] == jnp.arange(n_exp)[None, :]
    tile_in_group = tiles - jnp.sum(jnp.where(own, (tile_end - ntile)[None, :], 0), axis=1)
    group_rows = jnp.sum(jnp.where(own, cnt[None, :], 0), axis=1)
    tile_rows = jnp.where(valid, jnp.clip(group_rows - tile_in_group * tm, 0, tm), 0)
    return pos.reshape(-1).astype(jnp.int32), te, tile_rows.astype(jnp.int32)


def _moe_and_final(attn, h, w_o, g_ffn, w_router, b_router, w_gu, w_down, p_layers, layer, g_ple, w_pgate,
                   w_pproj, g_final, *, tm_expert):
    m, d = h.shape
    n_exp = w_router.shape[1]
    h4, hn, route, counts = _oproj_router(attn, h, w_o, g_ffn, w_router, b_router)
    n_tiles = -(-(m * TOP_K) // tm_expert) + n_exp
    pos, te, tv = _route_plan(route, counts, n_exp, tm_expert, n_tiles)
    xs = _dispatch(hn, pos, tv, tm_expert)
    ys = _experts(xs, te, tv, w_gu, w_down, tm_expert)
    weights = route[2:2 + TOP_K].T
    return _combine(ys, pos, h4, weights, p_layers, layer, g_ple, w_pgate, w_pproj, g_final)


def kernel(x_prompt, x_sample, state_h, state_conv, cache_k, cache_v, cache_logf, page_table, p_prompt, p_sample, g_mix_a, w_in_a, conv_w_a, conv_b_a, w_rg_a, b_rg_a, w_ig_a, b_ig_a, lru_lambda_a, w_out_a, g_kv, w_kv, b_f, g_mix_b, w_q_b, w_o_b, g_ffn, w_gu_dense, w_down_dense, w_router, b_router, w_gu_moe, w_down_moe, g_ple, w_ple_gate, w_ple_proj, g_final):
    assert w_in_a.shape[0] == 1 and w_q_b.shape[0] == 1 and g_ffn.shape[0] == 2
    assert x_sample.shape[1] == 1
    bsz, t_len, d = x_prompt.shape
    dec_b = x_sample.shape[0]
    nh = b_f.shape[0]
    qk = w_q_b.shape[2]
    hd = qk // nh
    r = w_out_a.shape[1]
    scale = hd ** -0.5
    mp = bsz * t_len

    wins, rec_w = _rec_weights(w_in_a[0], conv_w_a[0], conv_b_a[0], w_rg_a[0], b_rg_a[0], w_ig_a[0],
                               b_ig_a[0], lru_lambda_a[0], w_out_a[0])
    w_gu_d = w_gu_dense[0].astype(BF16)
    w_down_d = w_down_dense[0].astype(BF16)
    w_pg = [w_ple_gate[l].astype(BF16) for l in range(2)]
    w_pp = [w_ple_proj[l].astype(BF16) for l in range(2)]

    def rest_of_step(h1, p, seq_len):
        h3 = _ffn_dense(h1, g_ffn[0], w_gu_d, w_down_d, p, 0, g_ple[0], w_pg[0], w_pp[0])
        return h3, _kvq(h3, g_kv, g_mix_b[0], w_kv, b_f, w_q_b[0], seq_len=seq_len)

    def moe(attn, h3, p, tm_expert):
        return _moe_and_final(attn, h3, w_o_b[0], g_ffn[1], w_router[0], b_router[0], w_gu_moe[0],
                              w_down_moe[0], p, 1, g_ple[1], w_pg[1], w_pp[1], g_final, tm_expert=tm_expert)

    h1, hl_p, cn_p = _rec_prompt(x_prompt, g_mix_a[0], wins, rec_w)
    pp = p_prompt.reshape(2, mp, -1)
    h3, (k_p, v_p, q_p, lft_p) = rest_of_step(h1.reshape(mp, d), pp, t_len)
    attn_p = _attn_prompt(q_p.reshape(bsz, t_len, qk), k_p.reshape(bsz, t_len, qk), v_p.reshape(bsz, t_len, qk),
                          lft_p, nh=nh, scale=scale)
    lf_p = jnp.swapaxes(lft_p, 1, 2)
    y_p = moe(attn_p.reshape(mp, qk), h3, pp, min(1024, mp))

    h1s, hl_s, cn_s = _rec_sample(x_sample.reshape(dec_b, d), g_mix_a[0], wins, rec_w, state_conv[0], state_h[0])
    ps = p_sample.reshape(2, dec_b, -1)
    h3s, (k_s, v_s, q_s, lf_s) = rest_of_step(h1s, ps, None)
    attn_s = _attn_sample(q_s.astype(F32), k_s, v_s, lf_s, cache_k, cache_v, cache_logf, page_table, scale=scale)
    y_s = moe(attn_s, h3s, ps, dec_b)

    return (y_p.reshape(bsz, t_len, d), y_s.reshape(dec_b, 1, d),
            hl_p[None], cn_p[None],
            k_p.reshape(bsz, t_len, nh, hd), v_p.reshape(bsz, t_len, nh, hd), lf_p,
            hl_s[None], cn_s[None],
            k_s.reshape(dec_b, 1, nh, hd), v_s.reshape(dec_b, 1, nh, hd), lf_s.reshape(dec_b, 1, nh))
```
